```python
import math
import jax
import jax.numpy as jnp
from jax import lax
import numpy as np


D_MODEL = 2048
BATCH = 8
SEQ = 2048
DEPTH = 4

N_META = 16
GLA_HEADS = 4
GLA_DK = 64
GLA_DV = 128
GLA_WIDTH = GLA_HEADS * GLA_DV
GLA_RANK = 16
GLA_TAU = 16.0
GLA_CHUNK = 64
GLA_PAD = (-N_META) % GLA_CHUNK
SWA_HEADS = 8
SWA_KV_HEADS = 2
SWA_GROUP = SWA_HEADS // SWA_KV_HEADS
SWA_HEAD_DIM = 128
SWA_WIDTH = SWA_HEADS * SWA_HEAD_DIM
WINDOW = 128
SWA_BLOCK = 128
SWA_PAD = (-N_META) % SWA_BLOCK
HY_WIDTH = 512
HY_SHORT = 3
HY_BANDS = 16
HY_EMB = 2 * HY_BANDS + 1
HY_FFN = 64
HY_FAST_PCT = 0.3
HY_SLOW_PCT = 1.5
HY_TARGET = 1e-2
MIX_WIDTH = GLA_WIDTH + SWA_WIDTH + HY_WIDTH
IN_SIZES = (GLA_HEADS * GLA_DK, GLA_HEADS * GLA_DK, GLA_WIDTH, GLA_RANK, GLA_RANK, GLA_WIDTH,
            SWA_WIDTH, SWA_KV_HEADS * SWA_HEAD_DIM, SWA_KV_HEADS * SWA_HEAD_DIM, 3 * HY_WIDTH)
IN_COLS = sum(IN_SIZES)
IN_OFFSETS = tuple(int(o) for o in np.cumsum(IN_SIZES)[:-1])
N_GROUPS = 4
EXPERTS_PER_GROUP = 4
N_EXPERTS = N_GROUPS * EXPERTS_PER_GROUP
TOP_K = 2
D_EXPERT = 1024
ALPHA = (2.0 * DEPTH) ** 0.25
BETA = (8.0 * DEPTH) ** -0.25
LN_EPS = 1e-5
RMS_EPS = 1e-6
NEG = -1e30

kernel_name = 'hymba_gla_swa_hyena_hmoe_encoder'


def layer_norm(x, g, b):
    xf = x.astype(jnp.float32)
    mu = jnp.mean(xf, axis=-1, keepdims=True)
    var = jnp.mean(jnp.square(xf - mu), axis=-1, keepdims=True)
    y = (xf - mu) * lax.rsqrt(var + LN_EPS) * g.astype(jnp.float32) + b.astype(jnp.float32)
    return y.astype(x.dtype)


def rms_norm(x, g):
    xf = x.astype(jnp.float32)
    return xf * lax.rsqrt(jnp.mean(xf * xf, axis=-1, keepdims=True) + RMS_EPS) * g.astype(jnp.float32)


def gla_chunked(q, k, v, log_a, include_diag):
    bsz, nh, t, dk = q.shape
    dv = v.shape[-1]
    nc = t // GLA_CHUNK
    q, k, log_a = (a.reshape(bsz, nh, nc, GLA_CHUNK, dk) for a in (q, k, log_a))
    v = v.reshape(bsz, nh, nc, GLA_CHUNK, dv)
    cum = jnp.cumsum(log_a, axis=3)
    cum_last = cum[:, :, :, -1:, :]
    q_dec = q * jnp.exp(cum)
    k_inv = k * jnp.exp(-cum)
    lower = jnp.tril(jnp.ones((GLA_CHUNK, GLA_CHUNK), bool), k=0 if include_diag else -1)
    att = jnp.where(lower, jnp.einsum('bhncd,bhnsd->bhncs', q_dec, k_inv), 0.0)
    o_intra = jnp.einsum('bhncs,bhnsv->bhncv', att, v)
    kv_chunk = jnp.einsum('bhnsd,bhnsv->bhndv', k * jnp.exp(cum_last - cum), v)
    decay_chunk = jnp.exp(cum_last[:, :, :, 0, :])

    def step(state, inp):
        dec, kv = inp
        return dec[..., None] * state + kv, state

    s0 = jnp.zeros((bsz, nh, dk, dv), jnp.float32)
    _, s_prev = lax.scan(step, s0, (jnp.moveaxis(decay_chunk, 2, 0), jnp.moveaxis(kv_chunk, 2, 0)))
    s_prev = jnp.moveaxis(s_prev, 0, 2)
    o_inter = jnp.einsum('bhncd,bhndv->bhncv', q_dec, s_prev)
    return (o_intra + o_inter).reshape(bsz, nh, t, dv)


def gla_branch(q, k, v, gl_f, gl_b, r, w2_f, b_f, w2_b, b_b, norm_g):
    bsz, seq_len, _ = q.shape
    dt = q.dtype

    def heads(a, d):
        a = a.astype(jnp.float32).reshape(bsz, seq_len, GLA_HEADS, d)
        a = jnp.pad(a, ((0, 0), (GLA_PAD, 0), (0, 0), (0, 0)))
        return a.transpose(0, 2, 1, 3)

    log_f = jax.nn.log_sigmoid(gl_f.astype(jnp.float32) @ w2_f.astype(jnp.float32) + b_f.astype(jnp.float32)) / GLA_TAU
    log_b = jax.nn.log_sigmoid(gl_b.astype(jnp.float32) @ w2_b.astype(jnp.float32) + b_b.astype(jnp.float32)) / GLA_TAU
    qh = heads(q, GLA_DK) * GLA_DK ** -0.5
    kh = heads(k, GLA_DK)
    vh = heads(v, GLA_DV)
    o_fwd = gla_chunked(qh, kh, vh, heads(log_f, GLA_DK), True)

    def flip(a):
        return jnp.flip(a, axis=2)

    o_bwd = flip(gla_chunked(flip(qh), flip(kh), flip(vh), flip(heads(log_b, GLA_DK)), False))
    o = (o_fwd + o_bwd)[:, :, GLA_PAD:].transpose(0, 2, 1, 3)
    o = rms_norm(o, norm_g.reshape(GLA_HEADS, GLA_DV)).reshape(bsz, seq_len, GLA_WIDTH)
    return (o * jax.nn.silu(r.astype(jnp.float32))).astype(dt)


def swa_branch(q, k, v, sink, norm_g):
    bsz, seq_len, _ = q.shape
    dt = q.dtype
    t_pad = seq_len + SWA_PAD
    nb = t_pad // SWA_BLOCK
    qf = q.astype(jnp.float32).reshape(bsz, seq_len, SWA_KV_HEADS, SWA_GROUP, SWA_HEAD_DIM)
    qf = jnp.pad(qf, ((0, 0), (SWA_PAD, 0), (0, 0), (0, 0), (0, 0)))
    qb = qf.reshape(bsz, nb, SWA_BLOCK, SWA_KV_HEADS, SWA_GROUP, SWA_HEAD_DIM) * SWA_HEAD_DIM ** -0.5
    kf = k.astype(jnp.float32).reshape(bsz, seq_len, SWA_KV_HEADS, SWA_HEAD_DIM)
    vf = v.astype(jnp.float32).reshape(bsz, seq_len, SWA_KV_HEADS, SWA_HEAD_DIM)
    k_meta, v_meta = kf[:, :N_META], vf[:, :N_META]
    ext = ((0, 0), (SWA_PAD + SWA_BLOCK, SWA_BLOCK), (0, 0), (0, 0))

    def band(a):
        a = jnp.pad(a, ext).reshape(bsz, nb + 2, SWA_BLOCK, SWA_KV_HEADS, SWA_HEAD_DIM)
        return jnp.concatenate([a[:, :-2], a[:, 1:-1], a[:, 2:]], axis=2)

    k_band, v_band = band(kf), band(vf)
    pos_q = (jnp.arange(t_pad, dtype=jnp.int32) - SWA_PAD).reshape(nb, SWA_BLOCK)
    pos_e = (jnp.arange(t_pad + 2 * SWA_BLOCK, dtype=jnp.int32) - SWA_PAD - SWA_BLOCK).reshape(nb + 2, SWA_BLOCK)
    pos_band = jnp.concatenate([pos_e[:-2], pos_e[1:-1], pos_e[2:]], axis=1)
    dist_band = jnp.abs(pos_q[:, :, None] - pos_band[:, None, :])
    band_ok = (pos_band[:, None, :] >= N_META) & (pos_band[:, None, :] < seq_len) & (dist_band <= WINDOW)
    dist_meta = jnp.abs(pos_q[:, :, None] - jnp.arange(N_META, dtype=jnp.int32)[None, None, :]).astype(jnp.float32)
    slopes = (2.0 ** (-8.0 * jnp.arange(1, SWA_HEADS + 1, dtype=jnp.float32) / SWA_HEADS)).reshape(SWA_KV_HEADS, SWA_GROUP)
    slopes = slopes[:, :, None, None, None]
    s_meta = jnp.einsum('bnqhgd,bmhd->bhgnqm', qb, k_meta) - slopes * dist_meta
    s_band = jnp.einsum('bnqhgd,bnkhd->bhgnqk', qb, k_band) - slopes * dist_band.astype(jnp.float32)
    s_band = jnp.where(band_ok, s_band, NEG)
    sink_col = jnp.broadcast_to(sink.astype(jnp.float32).reshape(SWA_KV_HEADS, SWA_GROUP)[:, :, None, None, None],
                                (bsz, SWA_KV_HEADS, SWA_GROUP, nb, SWA_BLOCK, 1))
    p = jax.nn.softmax(jnp.concatenate([sink_col, s_meta, s_band], axis=-1), axis=-1)
    out = (jnp.einsum('bhgnqm,bmhd->bnqhgd', p[..., 1:1 + N_META], v_meta)
           + jnp.einsum('bhgnqk,bnkhd->bnqhgd', p[..., 1 + N_META:], v_band))
    out = out.reshape(bsz, t_pad, SWA_WIDTH)[:, SWA_PAD:]
    return rms_norm(out, norm_g).astype(dt)


def short_conv(u, w, b):
    up = jnp.pad(u, ((0, 0), (1, 1), (0, 0)))
    return up[:, :-2] * w[0] + up[:, 1:-1] * w[1] + up[:, 2:] * w[2] + b


def hyena_filters(seq_len, w1, b1, freq, w2, b2, w3):
    f32 = jnp.float32
    t = jnp.linspace(0.0, 1.0, seq_len, dtype=f32)[:, None]
    w = 2.0 * math.pi * jnp.arange(seq_len, dtype=f32)[:, None] / seq_len
    bands = jnp.linspace(1e-4, HY_BANDS - 1, HY_BANDS, dtype=f32)
    z = jnp.concatenate([t, jnp.cos(bands * w), -jnp.sin(bands * w)], axis=-1)
    freq = freq.astype(f32)
    h = jnp.sin(freq[0] * (z @ w1.astype(f32) + b1.astype(f32)))
    h = jnp.sin(freq[1] * (h @ w2.astype(f32) + b2.astype(f32)))
    h = h @ w3.astype(f32)
    max_decay = math.log(HY_TARGET) / HY_FAST_PCT
    min_decay = math.log(HY_TARGET) / HY_SLOW_PCT
    deltas = jnp.linspace(min_decay, max_decay, HY_WIDTH, dtype=f32)
    window = jnp.exp(-t * jnp.abs(deltas))
    h = h.reshape(seq_len, 2, HY_WIDTH) * window[:, None, :]
    return h[:, 0], h[:, 1]


def bidir_fftconv(u, h_fwd, h_bwd, skip):
    seq_len = u.shape[1]
    kern = jnp.concatenate([h_fwd, jnp.zeros((1, HY_WIDTH), jnp.float32), h_bwd[1:][::-1]], axis=0)
    u_hat = jnp.fft.rfft(u, n=2 * seq_len, axis=1)
    k_hat = jnp.fft.rfft(kern, axis=0)
    y = jnp.fft.irfft(u_hat * k_hat[None], n=2 * seq_len, axis=1)[:, :seq_len]
    return y + u * skip.astype(jnp.float32)


def hyena_branch(u, conv_w, conv_b, w1, b1, freq, w2, b2, w3, skip, norm_g):
    dt = u.dtype
    seq_len = u.shape[1]
    uc = short_conv(u.astype(jnp.float32), conv_w.astype(jnp.float32), conv_b.astype(jnp.float32))
    x0, x1, v = jnp.split(uc, 3, axis=-1)
    h_fwd, h_bwd = hyena_filters(seq_len, w1, b1, freq, w2, b2, w3)
    y = x0 * bidir_fftconv(x1 * v, h_fwd, h_bwd, skip)
    return rms_norm(y, norm_g).astype(dt)


def hier_moe(x, wg, bg, we, be, w_gate, w_up, w_down):
    bsz, seq_len, d = x.shape
    xt = x.reshape(-1, d)
    g_prob = jax.nn.softmax((xt @ wg + bg).astype(jnp.float32), axis=-1)
    g_top, g_idx = lax.top_k(g_prob, 1)
    e_logits = (xt @ we + be).astype(jnp.float32).reshape(-1, N_GROUPS, EXPERTS_PER_GROUP)
    e_logits = jnp.take_along_axis(e_logits, g_idx[:, :, None], axis=1)[:, 0]
    e_top, e_idx = lax.top_k(e_logits, TOP_K)
    e_w = jax.nn.softmax(e_top, axis=-1) * g_top
    expert_id = g_idx * EXPERTS_PER_GROUP + e_idx
    gates = jnp.sum(jax.nn.one_hot(expert_id, N_EXPERTS, dtype=jnp.float32) * e_w[..., None], axis=1)
    out = jnp.zeros((xt.shape[0], d), jnp.float32)
    for e in range(N_EXPERTS):
        h = jax.nn.silu(xt @ w_gate[e]) * (xt @ w_up[e])
        out = out + gates[:, e:e + 1] * (h @ w_down[e]).astype(jnp.float32)
    return out.astype(x.dtype).reshape(bsz, seq_len, d)


def setup_inputs(seed: int = 0):
    key = jax.random.key(seed)
    keys = iter(jax.random.split(key, 48))

    def nrm(shape, scale):
        return scale * jax.random.normal(next(keys), shape, jnp.float32)

    def gain(shape):
        return 1.0 + nrm(shape, 0.02)

    segments = ((256, 1.0), (256, 1.0), (GLA_WIDTH, BETA), (GLA_RANK, 1.0), (GLA_RANK, 1.0), (GLA_WIDTH, 1.0),
                (SWA_WIDTH, 1.0), (256, 1.0), (256, BETA), (2 * HY_WIDTH, 1.0), (HY_WIDTH, BETA))
    col_scale = jnp.concatenate([jnp.full((n,), s, jnp.float32) for n, s in segments])
    return {
        'x': nrm((BATCH, SEQ, D_MODEL), 1.0),
        'meta': nrm((N_META, D_MODEL), 1.0),
        'emb_ln_g': gain((D_MODEL,)),
        'emb_ln_b': nrm((D_MODEL,), 0.02),
        'w_in': nrm((DEPTH, D_MODEL, IN_COLS), D_MODEL ** -0.5) * col_scale,
        'b_in': nrm((DEPTH, IN_COLS), 0.02),
        'gla_w2_f': nrm((DEPTH, GLA_RANK, GLA_HEADS * GLA_DK), GLA_RANK ** -0.5),
        'gla_b_f': nrm((DEPTH, GLA_HEADS * GLA_DK), 0.02),
        'gla_w2_b': nrm((DEPTH, GLA_RANK, GLA_HEADS * GLA_DK), GLA_RANK ** -0.5),
        'gla_b_b': nrm((DEPTH, GLA_HEADS * GLA_DK), 0.02),
        'gla_norm_g': gain((DEPTH, GLA_WIDTH)),
        'swa_sink': nrm((DEPTH, SWA_HEADS), 0.5),
        'swa_norm_g': gain((DEPTH, SWA_WIDTH)),
        'hy_conv_w': nrm((DEPTH, HY_SHORT, 3 * HY_WIDTH), HY_SHORT ** -0.5),
        'hy_conv_b': nrm((DEPTH, 3 * HY_WIDTH), 0.02),
        'hy_w1': nrm((DEPTH, HY_EMB, HY_FFN), HY_EMB ** -0.5),
        'hy_b1': nrm((DEPTH, HY_FFN), 0.02),
        'hy_freq': gain((DEPTH, 2, HY_FFN)),
        'hy_w2': nrm((DEPTH, HY_FFN, HY_FFN), HY_FFN ** -0.5),
        'hy_b2': nrm((DEPTH, HY_FFN), 0.02),
        'hy_w3': nrm((DEPTH, HY_FFN, 2 * HY_WIDTH), 0.1 * HY_FFN ** -0.5),
        'hy_skip': nrm((DEPTH, HY_WIDTH), 0.5),
        'hy_norm_g': gain((DEPTH, HY_WIDTH)),
        'w_out': nrm((DEPTH, MIX_WIDTH, D_MODEL), BETA * MIX_WIDTH ** -0.5),
        'b_out': nrm((DEPTH, D_MODEL), 0.02),
        'ln1_g': gain((DEPTH, D_MODEL)),
        'ln1_b': nrm((DEPTH, D_MODEL), 0.02),
        'router_wg': nrm((DEPTH, D_MODEL, N_GROUPS), D_MODEL ** -0.5),
        'router_bg': nrm((DEPTH, N_GROUPS), 0.01),
        'router_we': nrm((DEPTH, D_MODEL, N_EXPERTS), D_MODEL ** -0.5),
        'router_be': nrm((DEPTH, N_EXPERTS), 0.01),
        'exp_w_gate': nrm((DEPTH, N_EXPERTS, D_MODEL, D_EXPERT), BETA * D_MODEL ** -0.5),
        'exp_w_up': nrm((DEPTH, N_EXPERTS, D_MODEL, D_EXPERT), BETA * D_MODEL ** -0.5),
        'exp_w_down': nrm((DEPTH, N_EXPERTS, D_EXPERT, D_MODEL), BETA * D_EXPERT ** -0.5),
        'ln2_g': gain((DEPTH, D_MODEL)),
        'ln2_b': nrm((DEPTH, D_MODEL), 0.02),
    }


def reference(x, meta, emb_ln_g, emb_ln_b, w_in, b_in, gla_w2_f, gla_b_f, gla_w2_b, gla_b_b, gla_norm_g,
              swa_sink, swa_norm_g, hy_conv_w, hy_conv_b, hy_w1, hy_b1, hy_freq, hy_w2, hy_b2, hy_w3, hy_skip,
              hy_norm_g, w_out, b_out, ln1_g, ln1_b, router_wg, router_bg, router_we, router_be,
              exp_w_gate, exp_w_up, exp_w_down, ln2_g, ln2_b):
    bsz = x.shape[0]
    dt = x.dtype
    meta_b = jnp.broadcast_to(meta.astype(dt)[None], (bsz, N_META, D_MODEL))
    h = layer_norm(jnp.concatenate([meta_b, x], axis=1), emb_ln_g, emb_ln_b)
    for l in range(DEPTH):
        u = h @ w_in[l] + b_in[l]
        a_q, a_k, a_v, a_gf, a_gb, a_r, b_q, b_k, b_v, c_u = jnp.split(u, list(IN_OFFSETS), axis=-1)
        y_a = gla_branch(a_q, a_k, a_v, a_gf, a_gb, a_r, gla_w2_f[l], gla_b_f[l], gla_w2_b[l], gla_b_b[l], gla_norm_g[l])
        y_b = swa_branch(b_q, b_k, b_v, swa_sink[l], swa_norm_g[l])
        y_c = hyena_branch(c_u, hy_conv_w[l], hy_conv_b[l], hy_w1[l], hy_b1[l], hy_freq[l], hy_w2[l], hy_b2[l],
                           hy_w3[l], hy_skip[l], hy_norm_g[l])
        mix = jnp.concatenate([y_a, y_b, y_c], axis=-1) @ w_out[l] + b_out[l]
        h = layer_norm(ALPHA * h + mix, ln1_g[l], ln1_b[l])
        moe = hier_moe(h, router_wg[l], router_bg[l], router_we[l], router_be[l],
                       exp_w_gate[l], exp_w_up[l], exp_w_down[l])
        h = layer_norm(ALPHA * h + moe, ln2_g[l], ln2_b[l])
    return h[:, N_META:]
```

```python
import functools
import math

import jax
import jax.numpy as jnp
import numpy as np
from jax import lax
from jax.experimental import pallas as pl
from jax.experimental.pallas import tpu as pltpu

F32 = jnp.float32
BF16 = jnp.bfloat16
HIGHEST = lax.Precision.HIGHEST

D_MODEL = 2048
N_META = 16
GLA_HEADS = 4
GLA_DK = 64
GLA_DV = 128
GLA_WIDTH = GLA_HEADS * GLA_DV
GLA_RANK = 16
GLA_TAU = 16.0
GLA_CHUNK = 64
SWA_HEADS = 8
SWA_KV_HEADS = 2
SWA_GROUP = SWA_HEADS // SWA_KV_HEADS
SWA_HEAD_DIM = 128
SWA_WIDTH = SWA_HEADS * SWA_HEAD_DIM
WINDOW = 128
HY_WIDTH = 512
HY_BANDS = 16
HY_EMB = 2 * HY_BANDS + 1
HY_FFN = 64
HY_FAST_PCT = 0.3
HY_SLOW_PCT = 1.5
HY_TARGET = 1e-2
N_GROUPS = 4
EXPERTS_PER_GROUP = 4
N_EXPERTS = N_GROUPS * EXPERTS_PER_GROUP
D_EXPERT = 1024
LN_EPS = 1e-5
RMS_EPS = 1e-6
NEG = -1e30

LANES = 128
A_COLS = 1664
GATE_COL0 = 1536
ROW_TILE = 384
MOE_TILE = 256
VMEM_LIMIT = 56 * 1024 * 1024


def _cparams(sem):
    return pltpu.CompilerParams(dimension_semantics=sem, vmem_limit_bytes=VMEM_LIMIT)


def _resident(shape, index_map):
    return pl.BlockSpec(shape, index_map, pipeline_mode=pl.Buffered(1))


def _dot(a, b, **kw):
    return jnp.dot(a, b, preferred_element_type=F32, **kw)


def _dot_nt(a, b):
    return lax.dot_general(a, b, (((1,), (1,)), ((), ())), preferred_element_type=F32)


def _ln(x, g, b):
    mu = jnp.mean(x, axis=-1, keepdims=True)
    xc = x - mu
    var = jnp.mean(xc * xc, axis=-1, keepdims=True)
    return xc * lax.rsqrt(var + LN_EPS) * g + b


def _ln_rows_kernel(x_ref, g_ref, b_ref, o_ref, ob_ref):
    y = _ln(x_ref[...], g_ref[...], b_ref[...])
    o_ref[...] = y
    ob_ref[...] = y.astype(BF16)


def ln_rows(x, g, b):
    t, d = x.shape
    row = pl.BlockSpec((ROW_TILE, d), lambda i: (i, 0))
    vec = pl.BlockSpec((1, d), lambda i: (0, 0))
    return pl.pallas_call(
        _ln_rows_kernel,
        grid=(t // ROW_TILE,),
        in_specs=[row, vec, vec],
        out_specs=[row, row],
        out_shape=[jax.ShapeDtypeStruct((t, d), F32), jax.ShapeDtypeStruct((t, d), BF16)],
        compiler_params=_cparams(("parallel",)),
        name="ln_rows",
    )(x, g.reshape(1, d), b.reshape(1, d))


def _proj_kernel(x_ref, w_ref, b_ref, o_ref):
    o_ref[...] = (_dot(x_ref[...], w_ref[...]) + b_ref[...]).astype(o_ref.dtype)


def project(x, w, b, out_dtype):
    t, k = x.shape
    n = w.shape[1]
    return pl.pallas_call(
        _proj_kernel,
        grid=(t // ROW_TILE,),
        in_specs=[pl.BlockSpec((ROW_TILE, k), lambda i: (i, 0)),
                  _resident((k, n), lambda i: (0, 0)),
                  pl.BlockSpec((1, n), lambda i: (0, 0))],
        out_specs=pl.BlockSpec((ROW_TILE, n), lambda i: (i, 0)),
        out_shape=jax.ShapeDtypeStruct((t, n), out_dtype),
        compiler_params=_cparams(("parallel",)),
        name="in_proj",
    )(x, w, b)


def _log_sigmoid(x):
    return jnp.minimum(x, 0.0) - jnp.log(1.0 + jnp.exp(-jnp.abs(x)))


def _chunk_scan(x, reverse):
    c = x.shape[0]
    idx = lax.broadcasted_iota(jnp.int32, x.shape, 0)
    s = 1
    while s < c:
        if reverse:
            x = x + jnp.where(idx < c - s, pltpu.roll(x, c - s, axis=0), 0.0)
        else:
            x = x + jnp.where(idx >= s, pltpu.roll(x, s, axis=0), 0.0)
        s *= 2
    return x


def _gla_kernel(q_ref, k_ref, v_ref, r_ref, gt_ref, w2f_ref, bf_ref, w2b_ref, bb_ref, ng_ref, o_ref,
                qs, ks, vs, lfs, lbs, acc, st):
    seq = q_ref.shape[1]
    seq_pad = qs.shape[0]
    n_chunks = seq_pad // GLA_CHUNK
    tail = seq_pad - seq
    c = GLA_CHUNK

    gates = gt_ref[0]
    log_f = _log_sigmoid(_dot(gates, w2f_ref[...], precision=HIGHEST) + bf_ref[...]) * (1.0 / GLA_TAU)
    log_b = _log_sigmoid(_dot(gates, w2b_ref[...], precision=HIGHEST) + bb_ref[...]) * (1.0 / GLA_TAU)
    for dst, src in ((qs, q_ref[0]), (ks, k_ref[0]), (vs, v_ref[0]), (lfs, log_f), (lbs, log_b)):
        dst[0:seq, :] = src
        dst[seq:seq_pad, :] = jnp.zeros((tail, dst.shape[1]), F32)

    lane = lax.broadcasted_iota(jnp.int32, (c, 2 * GLA_DK), 1)
    row = lax.broadcasted_iota(jnp.int32, (c, c), 0)
    col = lax.broadcasted_iota(jnp.int32, (c, c), 1)
    scale = GLA_DK ** -0.5

    def direction(log_ref, reverse, first):
        st[...] = jnp.zeros(st.shape, F32)
        keep = (col > row) if reverse else (col <= row)

        def body(i, carry):
            n = (n_chunks - 1 - i) if reverse else i
            r0 = pl.multiple_of(n * c, c)
            la = log_ref[pl.ds(r0, c), :]
            cum = _chunk_scan(la, reverse)
            tot = cum[0:1, :] if reverse else cum[c - 1:c, :]
            q = qs[pl.ds(r0, c), :]
            k = ks[pl.ds(r0, c), :]
            q_dec = q * jnp.exp(cum) * scale
            k_inv = (k * jnp.exp(-cum)).astype(BF16)
            k_dec = (k * jnp.exp(tot - cum)).astype(BF16)
            decay = jnp.exp(tot)
            for h in range(2):
                head_lanes = (lane < GLA_DK) if h == 0 else (lane >= GLA_DK)
                qm = jnp.where(head_lanes, q_dec, 0.0).astype(BF16)
                att = jnp.where(keep, _dot_nt(qm, k_inv), 0.0)
                vh = vs[pl.ds(r0, c), h * GLA_DV:(h + 1) * GLA_DV]
                state = st[h]
                o = _dot(att.astype(BF16), vh.astype(BF16)) + _dot_nt(qm, state.astype(BF16))
                st[h] = state * decay + _dot(vh.T.astype(BF16), k_dec)
                if first:
                    acc[pl.ds(r0, c), h * GLA_DV:(h + 1) * GLA_DV] = o
                else:
                    acc[pl.ds(r0, c), h * GLA_DV:(h + 1) * GLA_DV] += o
            return carry

        lax.fori_loop(0, n_chunks, body, 0)

    direction(lfs, False, True)
    direction(lbs, True, False)

    r = r_ref[0]
    gate = r * (1.0 / (1.0 + jnp.exp(-r)))
    for h in range(2):
        sl = slice(h * GLA_DV, (h + 1) * GLA_DV)
        o = acc[0:seq, sl]
        y = o * lax.rsqrt(jnp.mean(o * o, axis=-1, keepdims=True) + RMS_EPS) * ng_ref[:, sl]
        o_ref[0, :, sl] = (y * gate[:, sl]).astype(o_ref.dtype)


def gla_branch(u_a, w2f, b_f, w2b, b_b, norm_g):
    bsz, seq, _ = u_a.shape
    seq_pad = -(-seq // GLA_CHUNK) * GLA_CHUNK
    pair = 2 * GLA_DK

    def col(width, first_block):
        return pl.BlockSpec((1, seq, width), lambda b, p: (b, 0, first_block + p))

    vec128 = pl.BlockSpec((1, pair), lambda b, p: (0, p))
    return pl.pallas_call(
        _gla_kernel,
        grid=(bsz, GLA_HEADS // 2),
        in_specs=[col(pair, 0), col(pair, 2), col(2 * GLA_DV, 2), col(2 * GLA_DV, 4),
                  pl.BlockSpec((1, seq, LANES), lambda b, p: (b, 0, GATE_COL0 // LANES)),
                  pl.BlockSpec((LANES, pair), lambda b, p: (0, p)), vec128,
                  pl.BlockSpec((LANES, pair), lambda b, p: (0, p)), vec128,
                  pl.BlockSpec((1, 2 * GLA_DV), lambda b, p: (0, p))],
        out_specs=pl.BlockSpec((1, seq, 2 * GLA_DV), lambda b, p: (b, 0, p)),
        out_shape=jax.ShapeDtypeStruct((bsz, seq, GLA_WIDTH), BF16),
        scratch_shapes=[pltpu.VMEM((seq_pad, pair), F32), pltpu.VMEM((seq_pad, pair), F32),
                        pltpu.VMEM((seq_pad, 2 * GLA_DV), F32), pltpu.VMEM((seq_pad, pair), F32),
                        pltpu.VMEM((seq_pad, pair), F32), pltpu.VMEM((seq_pad, 2 * GLA_DV), F32),
                        pltpu.VMEM((2, GLA_DV, pair), F32)],
        compiler_params=_cparams(("parallel", "parallel")),
        name="gla",
    )(u_a, u_a, u_a, u_a, u_a, w2f, b_f.reshape(1, -1), w2b, b_b.reshape(1, -1), norm_g.reshape(1, -1))


def _swa_kernel(sink_ref, q_ref, k_ref, v_ref, ng_ref, o_ref):
    seq = q_ref.shape[1]
    blk = WINDOW
    n_blocks = (seq - N_META) // blk
    scale = SWA_HEAD_DIM ** -0.5
    hd = SWA_HEAD_DIM

    def attend(r0, nq, w0, nk):
        qi = lax.broadcasted_iota(jnp.int32, (nq, nk), 0)
        ki = lax.broadcasted_iota(jnp.int32, (nq, nk), 1)
        dist = jnp.abs(qi - ki + (r0 - w0))
        in_band = dist <= WINDOW
        dist = dist.astype(F32)
        qm = lax.broadcasted_iota(jnp.int32, (nq, blk), 0)
        km = lax.broadcasted_iota(jnp.int32, (nq, blk), 1)
        is_meta = km < N_META
        dist_meta = jnp.abs(qm - km + r0).astype(F32)
        outs = []
        for kv in range(SWA_KV_HEADS):
            ks = slice(kv * hd, (kv + 1) * hd)
            k_band = k_ref[0, pl.ds(w0, nk), ks]
            v_band = v_ref[0, pl.ds(w0, nk), ks]
            k_meta = k_ref[0, 0:blk, ks]
            v_meta = v_ref[0, 0:blk, ks]
            for g in range(SWA_GROUP):
                h = kv * SWA_GROUP + g
                slope = 2.0 ** (-8.0 * (h + 1) / SWA_HEADS)
                q = q_ref[0, pl.ds(r0, nq), h * hd:(h + 1) * hd]
                s_band = jnp.where(in_band, _dot_nt(q, k_band) * scale - slope * dist, NEG)
                s_meta = jnp.where(is_meta, _dot_nt(q, k_meta) * scale - slope * dist_meta, NEG)
                sink = sink_ref[h]
                m = jnp.maximum(jnp.maximum(jnp.max(s_band, axis=-1, keepdims=True),
                                            jnp.max(s_meta, axis=-1, keepdims=True)), sink)
                p_band = jnp.exp(s_band - m)
                p_meta = jnp.exp(s_meta - m)
                den = (jnp.sum(p_band, axis=-1, keepdims=True) + jnp.sum(p_meta, axis=-1, keepdims=True)
                       + jnp.exp(sink - m))
                o = _dot(p_band.astype(BF16), v_band) + _dot(p_meta.astype(BF16), v_meta)
                outs.append(o / den)
        out = jnp.concatenate(outs, axis=1)
        y = out * lax.rsqrt(jnp.mean(out * out, axis=-1, keepdims=True) + RMS_EPS) * ng_ref[...]
        o_ref[0, pl.ds(r0, nq), :] = y.astype(o_ref.dtype)

    attend(0, N_META, N_META, blk)

    last_w0 = N_META + (n_blocks - 3) * blk

    def body(j, carry):
        r0 = pl.multiple_of(N_META + j * blk, N_META)
        w0 = pl.multiple_of(jnp.clip(r0 - blk, N_META, last_w0), N_META)
        attend(r0, blk, w0, 3 * blk)
        return carry

    lax.fori_loop(0, n_blocks, body, 0)


def swa_branch(u_b, sink, norm_g):
    bsz, seq, _ = u_b.shape
    kvw = SWA_KV_HEADS * SWA_HEAD_DIM
    return pl.pallas_call(
        _swa_kernel,
        grid=(bsz,),
        in_specs=[pl.BlockSpec(memory_space=pltpu.SMEM),
                  pl.BlockSpec((1, seq, SWA_WIDTH), lambda b: (b, 0, 0)),
                  pl.BlockSpec((1, seq, kvw), lambda b: (b, 0, SWA_WIDTH // kvw)),
                  pl.BlockSpec((1, seq, kvw), lambda b: (b, 0, SWA_WIDTH // kvw + 1)),
                  pl.BlockSpec((1, SWA_WIDTH), lambda b: (0, 0))],
        out_specs=pl.BlockSpec((1, seq, SWA_WIDTH), lambda b: (b, 0, 0)),
        out_shape=jax.ShapeDtypeStruct((bsz, seq, SWA_WIDTH), BF16),
        compiler_params=_cparams(("parallel",)),
        name="swa",
    )(sink, u_b, u_b, u_b, norm_g.reshape(1, -1))


@functools.lru_cache(maxsize=None)
def _hyena_tables(seq):
    t = np.linspace(0.0, 1.0, seq, dtype=np.float32)[:, None]
    w = (np.float32(2.0 * math.pi) * np.arange(seq, dtype=np.float32)[:, None] / np.float32(seq)).astype(np.float32)
    bands = np.linspace(1e-4, HY_BANDS - 1, HY_BANDS, dtype=np.float32)
    arg = (bands * w).astype(np.float32).astype(np.float64)
    z = np.concatenate([t.astype(np.float64), np.cos(arg), -np.sin(arg)], axis=-1)
    z_pad = np.zeros((seq, LANES), np.float32)
    z_pad[:, :HY_EMB] = z.astype(np.float32)
    max_decay = math.log(HY_TARGET) / HY_FAST_PCT
    min_decay = math.log(HY_TARGET) / HY_SLOW_PCT
    rates = np.abs(np.linspace(min_decay, max_decay, HY_WIDTH, dtype=np.float32))[None, :]
    n = 2 * seq - 1
    fj = (np.arange(seq, dtype=np.int64)[:, None] * np.arange(seq, dtype=np.int64)[None, :]) % n
    ang = fj.astype(np.float64) * (2.0 * math.pi / n)
    return z_pad, rates.astype(np.float32), np.cos(ang).astype(np.float32), np.sin(ang).astype(np.float32)


def _hy_filter_kernel(z_ref, rate_ref, w1_ref, b1_ref, fr_ref, w2_ref, b2_ref, w3_ref, hs_ref, hd_ref):
    z = z_ref[...]
    h = jnp.sin(fr_ref[0, 0:1, :] * (_dot(z, w1_ref[0], precision=HIGHEST) + b1_ref[0]))
    h = jnp.sin(fr_ref[0, 1:2, :] * (_dot(h, w2_ref[0], precision=HIGHEST) + b2_ref[0]))
    h = _dot(h, w3_ref[0], precision=HIGHEST)
    window = jnp.exp(-z[:, 0:1] * rate_ref[...])
    h_fwd = h[:, :HY_WIDTH] * window
    h_bwd = h[:, HY_WIDTH:] * window
    first = lax.broadcasted_iota(jnp.int32, h_bwd.shape, 0) == 0
    h_bwd = jnp.where(first, 0.0, h_bwd)
    hs_ref[0] = h_fwd + h_bwd
    hd_ref[0] = h_bwd - h_fwd


def hyena_filters(z, rates, w1p, b1, freq, w2, b2, w3):
    depth = w1p.shape[0]
    seq = z.shape[0]

    def per_layer(*shape):
        return pl.BlockSpec((1,) + shape, lambda l: (l,) + (0,) * len(shape))

    out = jax.ShapeDtypeStruct((depth, seq, HY_WIDTH), F32)
    return pl.pallas_call(
        _hy_filter_kernel,
        grid=(depth,),
        in_specs=[pl.BlockSpec((seq, LANES), lambda l: (0, 0)), pl.BlockSpec((1, HY_WIDTH), lambda l: (0, 0)),
                  per_layer(LANES, HY_FFN), per_layer(1, HY_FFN), per_layer(2, HY_FFN),
                  per_layer(HY_FFN, HY_FFN), per_layer(1, HY_FFN), per_layer(HY_FFN, 2 * HY_WIDTH)],
        out_specs=[per_layer(seq, HY_WIDTH), per_layer(seq, HY_WIDTH)],
        out_shape=[out, out],
        compiler_params=_cparams(("parallel",)),
        name="hyena_filters",
    )(z, rates, w1p, b1, freq, w2, b2, w3)


def _split_bf16(x):
    hi = x.astype(BF16)
    return hi, (x - hi.astype(F32)).astype(BF16)


def _hy_spectrum_kernel(cos_ref, sin_ref, hs_ref, hd_ref, kc_ref, ks_ref):
    seq = cos_ref.shape[0]
    n = 2 * seq - 1
    f = lax.broadcasted_iota(jnp.int32, (seq, 1), 0)
    weight = jnp.where(f == 0, 1.0 / n, 2.0 / n)
    s_hi, s_lo = _split_bf16(hs_ref[0])
    d_hi, d_lo = _split_bf16(hd_ref[0])
    kc_ref[0] = (_dot(cos_ref[...], s_hi) + _dot(cos_ref[...], s_lo)) * weight
    ks_ref[0] = (_dot(sin_ref[...], d_hi) + _dot(sin_ref[...], d_lo)) * weight


def hyena_spectrum(cos_t, sin_t, h_sum, h_diff):
    depth, seq, _ = h_sum.shape
    half = HY_WIDTH // 2
    table = _resident((seq, seq), lambda l, c: (0, 0))
    blk = pl.BlockSpec((1, seq, half), lambda l, c: (l, 0, c))
    out = jax.ShapeDtypeStruct((depth, seq, HY_WIDTH), F32)
    return pl.pallas_call(
        _hy_spectrum_kernel,
        grid=(depth, 2),
        in_specs=[table, table, blk, blk],
        out_specs=[blk, blk],
        out_shape=[out, out],
        compiler_params=_cparams(("parallel", "parallel")),
        name="hyena_spectrum",
    )(cos_t, sin_t, h_sum, h_diff)


def _short_conv(u, w, b):
    seq = u.shape[0]
    row = lax.broadcasted_iota(jnp.int32, u.shape, 0)
    prev = jnp.where(row == 0, 0.0, pltpu.roll(u, 1, axis=0))
    nxt = jnp.where(row == seq - 1, 0.0, pltpu.roll(u, seq - 1, axis=0))
    return prev * w[0:1, :] + u * w[1:2, :] + nxt * w[2:3, :] + b


def _hyena_kernel(x0_ref, x1_ref, v_ref, w0_ref, w1_ref, wv_ref, b0_ref, b1_ref, bv_ref,
                  cos_ref, sin_ref, kc_ref, ks_ref, skip_ref, ng_ref, o_ref, acc):
    c = pl.program_id(1)
    half = x0_ref.shape[2]
    x0 = _short_conv(x0_ref[0], w0_ref[...], b0_ref[...])
    x1 = _short_conv(x1_ref[0], w1_ref[...], b1_ref[...])
    v = _short_conv(v_ref[0], wv_ref[...], bv_ref[...])
    g = x1 * v
    gb = g.astype(BF16)
    uc = _dot(cos_ref[...], gb)
    us = _dot(sin_ref[...], gb)
    kc = kc_ref[0]
    ks = ks_ref[0]
    pr = (uc * kc + us * ks).astype(BF16)
    pi = (uc * ks - us * kc).astype(BF16)
    y = _dot(cos_ref[...], pr) - _dot(sin_ref[...], pi) + g * skip_ref[...]
    y = x0 * y

    @pl.when(c == 0)
    def _():
        acc[:, 0:half] = y

    @pl.when(c == 1)
    def _():
        acc[:, half:2 * half] = y
        full = acc[...]
        o_ref[0] = (full * lax.rsqrt(jnp.mean(full * full, axis=-1, keepdims=True) + RMS_EPS)
                    * ng_ref[...]).astype(o_ref.dtype)


def hyena_branch(u_c, conv_w, conv_b, cos_t, sin_t, k_cos, k_sin, layer, skip, norm_g):
    bsz, seq, _ = u_c.shape
    half = HY_WIDTH // 2

    def stream(first_block):
        return pl.BlockSpec((1, seq, half), lambda b, c: (b, 0, first_block + c))

    def wcol(rows, first_block):
        return pl.BlockSpec((rows, half), lambda b, c: (0, first_block + c))

    table = _resident((seq, seq), lambda b, c: (0, 0))
    spec = pl.BlockSpec((1, seq, half), lambda b, c: (layer, 0, c))
    return pl.pallas_call(
        _hyena_kernel,
        grid=(bsz, 2),
        in_specs=[stream(0), stream(2), stream(4), wcol(3, 0), wcol(3, 2), wcol(3, 4),
                  wcol(1, 0), wcol(1, 2), wcol(1, 4), table, table, spec, spec, wcol(1, 0),
                  pl.BlockSpec((1, HY_WIDTH), lambda b, c: (0, 0))],
        out_specs=pl.BlockSpec((1, seq, HY_WIDTH), lambda b, c: (b, 0, 0)),
        out_shape=jax.ShapeDtypeStruct((bsz, seq, HY_WIDTH), BF16),
        scratch_shapes=[pltpu.VMEM((seq, HY_WIDTH), F32)],
        compiler_params=_cparams(("parallel", "arbitrary")),
        name="hyena",
    )(u_c, u_c, u_c, conv_w, conv_w, conv_w, conv_b, conv_b, conv_b, cos_t, sin_t, k_cos, k_sin,
      skip.reshape(1, -1), norm_g.reshape(1, -1))


def _outproj_kernel(alpha, ya_ref, yb_ref, yc_ref, h_ref, w_ref, b_ref, g_ref, be_ref, wr_ref,
                    h1_ref, h1b_ref, lg_ref):
    na = ya_ref.shape[1]
    nb = yb_ref.shape[1]
    mix = (_dot(ya_ref[...], w_ref[0:na, :]) + _dot(yb_ref[...], w_ref[na:na + nb, :])
           + _dot(yc_ref[...], w_ref[na + nb:, :]) + b_ref[...])
    h1 = _ln(alpha * h_ref[...] + mix, g_ref[...], be_ref[...])
    h1_ref[...] = h1
    h1b = h1.astype(BF16)
    h1b_ref[...] = h1b
    lg_ref[...] = _dot(h1b, wr_ref[...])


def out_projection(alpha, y_a, y_b, y_c, h, w, b, g, be, w_router):
    t, d = h.shape

    def row(width):
        return pl.BlockSpec((ROW_TILE, width), lambda i: (i, 0))

    def whole(r, c):
        return _resident((r, c), lambda i: (0, 0))

    return pl.pallas_call(
        functools.partial(_outproj_kernel, alpha),
        grid=(t // ROW_TILE,),
        in_specs=[row(y_a.shape[1]), row(y_b.shape[1]), row(y_c.shape[1]), row(d), whole(d, d),
                  whole(1, d), whole(1, d), whole(1, d), whole(d, LANES)],
        out_specs=[row(d), row(d), row(LANES)],
        out_shape=[jax.ShapeDtypeStruct((t, d), F32), jax.ShapeDtypeStruct((t, d), BF16),
                   jax.ShapeDtypeStruct((t, LANES), F32)],
        compiler_params=_cparams(("parallel",)),
        name="out_proj_ln",
    )(y_a, y_b, y_c, h, w, b.reshape(1, d), g.reshape(1, d), be.reshape(1, d), w_router)


def _first_index_of_max(x, valid, lane):
    m = jnp.max(jnp.where(valid, x, NEG), axis=-1, keepdims=True)
    idx = jnp.min(jnp.where(valid & (x == m), lane, float(LANES)), axis=-1, keepdims=True)
    return m, idx


def _route_kernel(lg_ref, b_ref, o_ref):
    x = lg_ref[...] + b_ref[...]
    lane = lax.broadcasted_iota(jnp.int32, x.shape, 1).astype(F32)
    is_group = lane < N_GROUPS
    gm, g_idx = _first_index_of_max(x, is_group, lane)
    g_top = 1.0 / jnp.sum(jnp.where(is_group, jnp.exp(x - gm), 0.0), axis=-1, keepdims=True)
    lo = N_GROUPS + g_idx * EXPERTS_PER_GROUP
    in_group = (lane >= lo) & (lane < lo + EXPERTS_PER_GROUP)
    m1, i1 = _first_index_of_max(x, in_group, lane)
    m2, i2 = _first_index_of_max(x, in_group & (lane != i1), lane)
    e2 = jnp.exp(m2 - m1)
    w1 = g_top / (1.0 + e2)
    w2 = g_top * e2 / (1.0 + e2)
    out = jnp.where(lane == 0, i1 - N_GROUPS,
                    jnp.where(lane == 1, i2 - N_GROUPS,
                              jnp.where(lane == 2, w1, jnp.where(lane == 3, w2, 0.0))))
    o_ref[...] = out


def route(logits, bias):
    t = logits.shape[0]
    row = pl.BlockSpec((ROW_TILE, LANES), lambda i: (i, 0))
    return pl.pallas_call(
        _route_kernel,
        grid=(t // ROW_TILE,),
        in_specs=[row, pl.BlockSpec((1, LANES), lambda i: (0, 0))],
        out_specs=row,
        out_shape=jax.ShapeDtypeStruct((t, LANES), F32),
        compiler_params=_cparams(("parallel",)),
        name="route",
    )(logits, bias)


def dispatch_plan(expert_ids, n_tiles):
    t = expert_ids.shape[0]
    flat = expert_ids.reshape(-1)
    onehot = (flat[:, None] == jnp.arange(N_EXPERTS, dtype=jnp.int32)[None, :]).astype(jnp.int32)
    counts = jnp.sum(onehot, axis=0)
    rank = jnp.sum((jnp.cumsum(onehot, axis=0) - onehot) * onehot, axis=1)
    padded = (counts + MOE_TILE - 1) // MOE_TILE * MOE_TILE
    ends = jnp.cumsum(padded)
    starts = ends - padded
    pos = starts[flat] + rank
    token = jnp.arange(2 * t, dtype=jnp.int32) // 2
    row_token = jnp.zeros((n_tiles * MOE_TILE,), jnp.int32).at[pos].set(token)
    n_used = (ends[-1] // MOE_TILE).astype(jnp.int32)
    tile_start = jnp.minimum(jnp.arange(n_tiles, dtype=jnp.int32), n_used - 1) * MOE_TILE
    tile_expert = jnp.sum((tile_start[:, None] >= ends[None, :]).astype(jnp.int32), axis=1)
    return pos.astype(jnp.int32), row_token, tile_expert.astype(jnp.int32), n_used.reshape(1)


def _moe_kernel(te_ref, nu_ref, rt_ref, x_hbm, wg_ref, wu_ref, wd_ref, o_ref, xbuf, sem):
    i = pl.program_id(0)
    n_used = nu_ref[0]
    tm = xbuf.shape[1]

    def row_copy(tile, slot, r):
        tok = rt_ref[tile * tm + r]
        return pltpu.make_async_copy(x_hbm.at[pl.ds(tok, 1)], xbuf.at[slot, pl.ds(r, 1)], sem.at[slot])

    def start_gather(tile, slot):
        def body(r, carry):
            row_copy(tile, slot, r).start()
            return carry
        lax.fori_loop(0, tm, body, 0)

    def wait_gather(slot):
        pltpu.make_async_copy(x_hbm.at[pl.ds(0, tm)], xbuf.at[slot], sem.at[slot]).wait()

    @pl.when(i == 0)
    def _():
        start_gather(0, 0)

    @pl.when(i + 1 < n_used)
    def _():
        start_gather(i + 1, (i + 1) % 2)

    @pl.when(i < n_used)
    def _():
        slot = i % 2
        wait_gather(slot)
        x = xbuf[slot].astype(BF16)
        a = _dot(x, wg_ref[0])
        u = _dot(x, wu_ref[0])
        hmid = (a * (1.0 / (1.0 + jnp.exp(-a))) * u).astype(BF16)
        o_ref[...] = _dot(hmid, wd_ref[0])

    @pl.when(i >= n_used)
    def _():
        o_ref[...] = jnp.zeros(o_ref.shape, o_ref.dtype)


def expert_mlps(x, w_gate, w_up, w_down, tile_expert, n_used, row_token):
    t, d = x.shape
    n_tiles = tile_expert.shape[0]
    f = w_gate.shape[2]
    grid_spec = pltpu.PrefetchScalarGridSpec(
        num_scalar_prefetch=3,
        grid=(n_tiles,),
        in_specs=[pl.BlockSpec(memory_space=pl.ANY),
                  pl.BlockSpec((1, d, f), lambda i, te, nu, rt: (te[i], 0, 0)),
                  pl.BlockSpec((1, d, f), lambda i, te, nu, rt: (te[i], 0, 0)),
                  pl.BlockSpec((1, f, d), lambda i, te, nu, rt: (te[i], 0, 0))],
        out_specs=pl.BlockSpec((MOE_TILE, d), lambda i, te, nu, rt: (i, 0)),
        scratch_shapes=[pltpu.VMEM((2, MOE_TILE, d), F32), pltpu.SemaphoreType.DMA((2,))],
    )
    return pl.pallas_call(
        _moe_kernel,
        grid_spec=grid_spec,
        out_shape=jax.ShapeDtypeStruct((n_tiles * MOE_TILE, d), F32),
        compiler_params=_cparams(("arbitrary",)),
        name="moe_experts",
    )(tile_expert, n_used, row_token, x, w_gate, w_up, w_down)


def _combine_kernel(alpha, pos_ref, y_hbm, h_ref, rw_ref, g_ref, b_ref, o_ref, ob_ref, ybuf, sem):
    i = pl.program_id(0)
    n = pl.num_programs(0)
    tm = h_ref.shape[0]

    def row_copy(tile, slot, r, k):
        src = pos_ref[(tile * tm + r) * 2 + k]
        return pltpu.make_async_copy(y_hbm.at[pl.ds(src, 1)], ybuf.at[slot, k, pl.ds(r, 1)], sem.at[slot])

    def for_rows(tile, slot, fn):
        def body(r, carry):
            fn(row_copy(tile, slot, r, 0))
            fn(row_copy(tile, slot, r, 1))
            return carry
        lax.fori_loop(0, tm, body, 0)

    @pl.when(i == 0)
    def _():
        for_rows(0, 0, lambda cp: cp.start())

    @pl.when(i + 1 < n)
    def _():
        for_rows(i + 1, (i + 1) % 2, lambda cp: cp.start())

    slot = i % 2
    for k in range(2):
        pltpu.make_async_copy(y_hbm.at[pl.ds(0, tm)], ybuf.at[slot, k], sem.at[slot]).wait()
    rw = rw_ref[...]
    moe = ybuf[slot, 0] * rw[:, 2:3] + ybuf[slot, 1] * rw[:, 3:4]
    y = _ln(alpha * h_ref[...] + moe, g_ref[...], b_ref[...])
    o_ref[...] = y
    ob_ref[...] = y.astype(BF16)


COMBINE_TILE = 384


def combine_ln(alpha, pos, y_sorted, h, route_out, g, b):
    t, d = h.shape
    tm = COMBINE_TILE
    grid_spec = pltpu.PrefetchScalarGridSpec(
        num_scalar_prefetch=1,
        grid=(t // tm,),
        in_specs=[pl.BlockSpec(memory_space=pl.ANY),
                  pl.BlockSpec((tm, d), lambda i, p: (i, 0)),
                  pl.BlockSpec((tm, LANES), lambda i, p: (i, 0)),
                  pl.BlockSpec((1, d), lambda i, p: (0, 0)),
                  pl.BlockSpec((1, d), lambda i, p: (0, 0))],
        out_specs=[pl.BlockSpec((tm, d), lambda i, p: (i, 0)), pl.BlockSpec((tm, d), lambda i, p: (i, 0))],
        scratch_shapes=[pltpu.VMEM((2, 2, tm, d), F32), pltpu.SemaphoreType.DMA((2,))],
    )
    return pl.pallas_call(
        functools.partial(_combine_kernel, alpha),
        grid_spec=grid_spec,
        out_shape=[jax.ShapeDtypeStruct((t, d), F32), jax.ShapeDtypeStruct((t, d), BF16)],
        compiler_params=_cparams(("arbitrary",)),
        name="moe_combine_ln",
    )(pos, y_sorted, h, route_out, g.reshape(1, d), b.reshape(1, d))


def kernel(x, meta, emb_ln_g, emb_ln_b, w_in, b_in, gla_w2_f, gla_b_f, gla_w2_b, gla_b_b, gla_norm_g,
           swa_sink, swa_norm_g, hy_conv_w, hy_conv_b, hy_w1, hy_b1, hy_freq, hy_w2, hy_b2, hy_w3, hy_skip,
           hy_norm_g, w_out, b_out, ln1_g, ln1_b, router_wg, router_bg, router_we, router_be,
           exp_w_gate, exp_w_up, exp_w_down, ln2_g, ln2_b):
    bsz, seq_in, d = x.shape
    depth = w_in.shape[0]
    seq = seq_in + N_META
    t = bsz * seq
    alpha = (2.0 * depth) ** 0.25

    o_gates, o_r, o_swa, o_hy = 1024, 1056, 1568, 3104
    pad_cols = A_COLS - (o_swa)
    w_a = jnp.concatenate([w_in[:, :, :o_gates], w_in[:, :, o_r:o_swa], w_in[:, :, o_gates:o_r],
                           jnp.zeros((depth, d, pad_cols), w_in.dtype)], axis=-1).astype(BF16)
    b_a = jnp.concatenate([b_in[:, :o_gates], b_in[:, o_r:o_swa], b_in[:, o_gates:o_r],
                           jnp.zeros((depth, pad_cols), b_in.dtype)], axis=-1)
    w_b = w_in[:, :, o_swa:o_hy].astype(BF16)
    w_c = w_in[:, :, o_hy:].astype(BF16)
    w_out_b = w_out.astype(BF16)
    w2f = jnp.zeros((depth, LANES, gla_w2_f.shape[2]), F32).at[:, :GLA_RANK].set(gla_w2_f)
    w2b = jnp.zeros((depth, LANES, gla_w2_b.shape[2]), F32).at[:, GLA_RANK:2 * GLA_RANK].set(gla_w2_b)
    n_r = N_GROUPS + N_EXPERTS
    w_router = jnp.zeros((depth, d, LANES), F32).at[:, :, :N_GROUPS].set(router_wg)
    w_router = w_router.at[:, :, N_GROUPS:n_r].set(router_we).astype(BF16)
    b_router = jnp.zeros((depth, 1, LANES), F32).at[:, 0, :N_GROUPS].set(router_bg)
    b_router = b_router.at[:, 0, N_GROUPS:n_r].set(router_be)
    wg_b = exp_w_gate.astype(BF16)
    wu_b = exp_w_up.astype(BF16)
    wd_b = exp_w_down.astype(BF16)

    z_np, rates_np, cos_np, sin_np = _hyena_tables(seq)
    cos_t = jnp.asarray(cos_np).astype(BF16)
    sin_t = jnp.asarray(sin_np).astype(BF16)
    w1p = jnp.zeros((depth, LANES, HY_FFN), F32).at[:, :HY_EMB].set(hy_w1)
    h_sum, h_diff = hyena_filters(jnp.asarray(z_np), jnp.asarray(rates_np), w1p, hy_b1.reshape(depth, 1, HY_FFN),
                                  hy_freq, hy_w2, hy_b2.reshape(depth, 1, HY_FFN), hy_w3)
    k_cos, k_sin = hyena_spectrum(cos_t, sin_t, h_sum, h_diff)

    n_tiles = (2 * t + N_EXPERTS * (MOE_TILE - 1)) // MOE_TILE + 1

    tokens = jnp.concatenate([jnp.broadcast_to(meta.astype(x.dtype)[None], (bsz, N_META, d)), x], axis=1)
    h, hb = ln_rows(tokens.reshape(t, d), emb_ln_g, emb_ln_b)
    for l in range(depth):
        u_a = project(hb, w_a[l], b_a[l].reshape(1, -1), F32).reshape(bsz, seq, -1)
        u_b = project(hb, w_b[l], b_in[l, o_swa:o_hy].reshape(1, -1), BF16).reshape(bsz, seq, -1)
        u_c = project(hb, w_c[l], b_in[l, o_hy:].reshape(1, -1), F32).reshape(bsz, seq, -1)
        y_a = gla_branch(u_a, w2f[l], gla_b_f[l], w2b[l], gla_b_b[l], gla_norm_g[l])
        y_b = swa_branch(u_b, swa_sink[l], swa_norm_g[l])
        y_c = hyena_branch(u_c, hy_conv_w[l], hy_conv_b[l].reshape(1, -1), cos_t, sin_t, k_cos, k_sin, l,
                           hy_skip[l], hy_norm_g[l])
        h1, h1b, logits = out_projection(alpha, y_a.reshape(t, -1), y_b.reshape(t, -1), y_c.reshape(t, -1), h,
                                         w_out_b[l], b_out[l], ln1_g[l], ln1_b[l], w_router[l])
        routed = route(logits, b_router[l])
        expert_ids = routed[:, 0:2].astype(jnp.int32)
        pos, row_token, tile_expert, n_used = dispatch_plan(expert_ids, n_tiles)
        y_sorted = expert_mlps(h1, wg_b[l], wu_b[l], wd_b[l], tile_expert, n_used, row_token)
        h, hb = combine_ln(alpha, pos, y_sorted, h1, routed, ln2_g[l], ln2_b[l])
    return h.reshape(bsz, seq, d)[:, N_META:]
```

```python
import functools
import math

import jax
import jax.numpy as jnp
import numpy as np
from jax import lax
from jax.experimental import pallas as pl
from jax.experimental.pallas import tpu as pltpu

F32 = jnp.float32
BF16 = jnp.bfloat16
HIGHEST = lax.Precision.HIGHEST

D_MODEL = 2048
N_META = 16
GLA_HEADS = 4
GLA_DK = 64
GLA_DV = 128
GLA_WIDTH = GLA_HEADS * GLA_DV
GLA_RANK = 16
GLA_TAU = 16.0
GLA_CHUNK = 64
SWA_HEADS = 8
SWA_KV_HEADS = 2
SWA_GROUP = SWA_HEADS // SWA_KV_HEADS
SWA_HEAD_DIM = 128
SWA_WIDTH = SWA_HEADS * SWA_HEAD_DIM
WINDOW = 128
HY_WIDTH = 512
HY_BANDS = 16
HY_EMB = 2 * HY_BANDS + 1
HY_FFN = 64
HY_FAST_PCT = 0.3
HY_SLOW_PCT = 1.5
HY_TARGET = 1e-2
N_GROUPS = 4
EXPERTS_PER_GROUP = 4
N_EXPERTS = N_GROUPS * EXPERTS_PER_GROUP
D_EXPERT = 1024
LN_EPS = 1e-5
RMS_EPS = 1e-6
NEG = -1e30

LANES = 128
A_COLS = 1664
GATE_COL0 = 1536
ROW_TILE = 384
MOE_TILE = 256
VMEM_LIMIT = 56 * 1024 * 1024


def _cparams(sem):
    return pltpu.CompilerParams(dimension_semantics=sem, vmem_limit_bytes=VMEM_LIMIT)


def _resident(shape, index_map):
    return pl.BlockSpec(shape, index_map, pipeline_mode=pl.Buffered(1))


def _dot(a, b, **kw):
    return jnp.dot(a, b, preferred_element_type=F32, **kw)


def _dot_nt(a, b):
    return lax.dot_general(a, b, (((1,), (1,)), ((), ())), preferred_element_type=F32)


def _ln(x, g, b):
    mu = jnp.mean(x, axis=-1, keepdims=True)
    xc = x - mu
    var = jnp.mean(xc * xc, axis=-1, keepdims=True)
    return xc * lax.rsqrt(var + LN_EPS) * g + b


def _ln_rows_kernel(x_ref, g_ref, b_ref, o_ref, ob_ref):
    y = _ln(x_ref[...], g_ref[...], b_ref[...])
    o_ref[...] = y
    ob_ref[...] = y.astype(BF16)


def ln_rows(x, g, b):
    t, d = x.shape
    row = pl.BlockSpec((ROW_TILE, d), lambda i: (i, 0))
    vec = pl.BlockSpec((1, d), lambda i: (0, 0))
    return pl.pallas_call(
        _ln_rows_kernel,
        grid=(t // ROW_TILE,),
        in_specs=[row, vec, vec],
        out_specs=[row, row],
        out_shape=[jax.ShapeDtypeStruct((t, d), F32), jax.ShapeDtypeStruct((t, d), BF16)],
        compiler_params=_cparams(("parallel",)),
        name="ln_rows",
    )(x, g.reshape(1, d), b.reshape(1, d))


def _proj_kernel(x_ref, w_ref, b_ref, o_ref):
    o_ref[...] = (_dot(x_ref[...], w_ref[...]) + b_ref[...]).astype(o_ref.dtype)


def project(x, w, b, out_dtype):
    t, k = x.shape
    n = w.shape[1]
    return pl.pallas_call(
        _proj_kernel,
        grid=(t // ROW_TILE,),
        in_specs=[pl.BlockSpec((ROW_TILE, k), lambda i: (i, 0)),
                  _resident((k, n), lambda i: (0, 0)),
                  pl.BlockSpec((1, n), lambda i: (0, 0))],
        out_specs=pl.BlockSpec((ROW_TILE, n), lambda i: (i, 0)),
        out_shape=jax.ShapeDtypeStruct((t, n), out_dtype),
        compiler_params=_cparams(("parallel",)),
        name="in_proj",
    )(x, w, b)


def _log_sigmoid(x):
    return jnp.minimum(x, 0.0) - jnp.log(1.0 + jnp.exp(-jnp.abs(x)))


def _chunk_scan(x, reverse):
    c = x.shape[0]
    idx = lax.broadcasted_iota(jnp.int32, x.shape, 0)
    s = 1
    while s < c:
        if reverse:
            x = x + jnp.where(idx < c - s, pltpu.roll(x, c - s, axis=0), 0.0)
        else:
            x = x + jnp.where(idx >= s, pltpu.roll(x, s, axis=0), 0.0)
        s *= 2
    return x


def _gla_kernel(q_ref, k_ref, v_ref, r_ref, gt_ref, w2f_ref, bf_ref, w2b_ref, bb_ref, ng_ref, o_ref,
                qs, ks, vs, lfs, lbs, acc, st):
    seq = q_ref.shape[1]
    seq_pad = qs.shape[0]
    n_chunks = seq_pad // GLA_CHUNK
    tail = seq_pad - seq
    c = GLA_CHUNK

    gates = gt_ref[0]
    log_f = _log_sigmoid(_dot(gates, w2f_ref[...], precision=HIGHEST) + bf_ref[...]) * (1.0 / GLA_TAU)
    log_b = _log_sigmoid(_dot(gates, w2b_ref[...], precision=HIGHEST) + bb_ref[...]) * (1.0 / GLA_TAU)
    for dst, src in ((qs, q_ref[0]), (ks, k_ref[0]), (vs, v_ref[0]), (lfs, log_f), (lbs, log_b)):
        dst[0:seq, :] = src
        dst[seq:seq_pad, :] = jnp.zeros((tail, dst.shape[1]), F32)

    lane = lax.broadcasted_iota(jnp.int32, (c, 2 * GLA_DK), 1)
    row = lax.broadcasted_iota(jnp.int32, (c, c), 0)
    col = lax.broadcasted_iota(jnp.int32, (c, c), 1)
    scale = GLA_DK ** -0.5

    def direction(log_ref, reverse, first):
        st[...] = jnp.zeros(st.shape, F32)
        keep = (col > row) if reverse else (col <= row)

        def body(i, carry):
            n = (n_chunks - 1 - i) if reverse else i
            r0 = pl.multiple_of(n * c, c)
            la = log_ref[pl.ds(r0, c), :]
            cum = _chunk_scan(la, reverse)
            tot = cum[0:1, :] if reverse else cum[c - 1:c, :]
            q = qs[pl.ds(r0, c), :]
            k = ks[pl.ds(r0, c), :]
            q_dec = q * jnp.exp(cum) * scale
            k_inv = (k * jnp.exp(-cum)).astype(BF16)
            k_dec = (k * jnp.exp(tot - cum)).astype(BF16)
            decay = jnp.exp(tot)
            for h in range(2):
                head_lanes = (lane < GLA_DK) if h == 0 else (lane >= GLA_DK)
                qm = jnp.where(head_lanes, q_dec, 0.0).astype(BF16)
                att = jnp.where(keep, _dot_nt(qm, k_inv), 0.0)
                vh = vs[pl.ds(r0, c), h * GLA_DV:(h + 1) * GLA_DV]
                state = st[h]
                o = _dot(att.astype(BF16), vh.astype(BF16)) + _dot_nt(qm, state.astype(BF16))
                st[h] = state * decay + _dot(vh.T.astype(BF16), k_dec)
                if first:
                    acc[pl.ds(r0, c), h * GLA_DV:(h + 1) * GLA_DV] = o
                else:
                    acc[pl.ds(r0, c), h * GLA_DV:(h + 1) * GLA_DV] += o
            return carry

        lax.fori_loop(0, n_chunks, body, 0)

    direction(lfs, False, True)
    direction(lbs, True, False)

    r = r_ref[0]
    gate = r * (1.0 / (1.0 + jnp.exp(-r)))
    for h in range(2):
        sl = slice(h * GLA_DV, (h + 1) * GLA_DV)
        o = acc[0:seq, sl]
        y = o * lax.rsqrt(jnp.mean(o * o, axis=-1, keepdims=True) + RMS_EPS) * ng_ref[:, sl]
        o_ref[0, :, sl] = (y * gate[:, sl]).astype(o_ref.dtype)


def gla_branch(u_a, w2f, b_f, w2b, b_b, norm_g):
    bsz, seq, _ = u_a.shape
    seq_pad = -(-seq // GLA_CHUNK) * GLA_CHUNK
    pair = 2 * GLA_DK

    def col(width, first_block):
        return pl.BlockSpec((1, seq, width), lambda b, p: (b, 0, first_block + p))

    vec128 = pl.BlockSpec((1, pair), lambda b, p: (0, p))
    return pl.pallas_call(
        _gla_kernel,
        grid=(bsz, GLA_HEADS // 2),
        in_specs=[col(pair, 0), col(pair, 2), col(2 * GLA_DV, 2), col(2 * GLA_DV, 4),
                  pl.BlockSpec((1, seq, LANES), lambda b, p: (b, 0, GATE_COL0 // LANES)),
                  pl.BlockSpec((LANES, pair), lambda b, p: (0, p)), vec128,
                  pl.BlockSpec((LANES, pair), lambda b, p: (0, p)), vec128,
                  pl.BlockSpec((1, 2 * GLA_DV), lambda b, p: (0, p))],
        out_specs=pl.BlockSpec((1, seq, 2 * GLA_DV), lambda b, p: (b, 0, p)),
        out_shape=jax.ShapeDtypeStruct((bsz, seq, GLA_WIDTH), BF16),
        scratch_shapes=[pltpu.VMEM((seq_pad, pair), F32), pltpu.VMEM((seq_pad, pair), F32),
                        pltpu.VMEM((seq_pad, 2 * GLA_DV), F32), pltpu.VMEM((seq_pad, pair), F32),
                        pltpu.VMEM((seq_pad, pair), F32), pltpu.VMEM((seq_pad, 2 * GLA_DV), F32),
                        pltpu.VMEM((2, GLA_DV, pair), F32)],
        compiler_params=_cparams(("parallel", "parallel")),
        name="gla",
    )(u_a, u_a, u_a, u_a, u_a, w2f, b_f.reshape(1, -1), w2b, b_b.reshape(1, -1), norm_g.reshape(1, -1))


def _swa_kernel(sink_ref, q_ref, k_ref, v_ref, ng_ref, o_ref):
    seq = q_ref.shape[1]
    blk = WINDOW
    n_blocks = (seq - N_META) // blk
    scale = SWA_HEAD_DIM ** -0.5
    hd = SWA_HEAD_DIM

    def attend(r0, nq, w0, nk):
        qi = lax.broadcasted_iota(jnp.int32, (nq, nk), 0)
        ki = lax.broadcasted_iota(jnp.int32, (nq, nk), 1)
        dist = jnp.abs(qi - ki + (r0 - w0))
        in_band = dist <= WINDOW
        dist = dist.astype(F32)
        qm = lax.broadcasted_iota(jnp.int32, (nq, blk), 0)
        km = lax.broadcasted_iota(jnp.int32, (nq, blk), 1)
        is_meta = km < N_META
        dist_meta = jnp.abs(qm - km + r0).astype(F32)
        outs = []
        for kv in range(SWA_KV_HEADS):
            ks = slice(kv * hd, (kv + 1) * hd)
            k_band = k_ref[0, pl.ds(w0, nk), ks]
            v_band = v_ref[0, pl.ds(w0, nk), ks]
            k_meta = k_ref[0, 0:blk, ks]
            v_meta = v_ref[0, 0:blk, ks]
            for g in range(SWA_GROUP):
                h = kv * SWA_GROUP + g
                slope = 2.0 ** (-8.0 * (h + 1) / SWA_HEADS)
                q = q_ref[0, pl.ds(r0, nq), h * hd:(h + 1) * hd]
                s_band = jnp.where(in_band, _dot_nt(q, k_band) * scale - slope * dist, NEG)
                s_meta = jnp.where(is_meta, _dot_nt(q, k_meta) * scale - slope * dist_meta, NEG)
                sink = sink_ref[h]
                m = jnp.maximum(jnp.maximum(jnp.max(s_band, axis=-1, keepdims=True),
                                            jnp.max(s_meta, axis=-1, keepdims=True)), sink)
                p_band = jnp.exp(s_band - m)
                p_meta = jnp.exp(s_meta - m)
                den = (jnp.sum(p_band, axis=-1, keepdims=True) + jnp.sum(p_meta, axis=-1, keepdims=True)
                       + jnp.exp(sink - m))
                o = _dot(p_band.astype(BF16), v_band) + _dot(p_meta.astype(BF16), v_meta)
                outs.append(o / den)
        out = jnp.concatenate(outs, axis=1)
        y = out * lax.rsqrt(jnp.mean(out * out, axis=-1, keepdims=True) + RMS_EPS) * ng_ref[...]
        o_ref[0, pl.ds(r0, nq), :] = y.astype(o_ref.dtype)

    attend(0, N_META, N_META, blk)

    last_w0 = N_META + (n_blocks - 3) * blk

    def body(j, carry):
        r0 = pl.multiple_of(N_META + j * blk, N_META)
        w0 = pl.multiple_of(jnp.clip(r0 - blk, N_META, last_w0), N_META)
        attend(r0, blk, w0, 3 * blk)
        return carry

    lax.fori_loop(0, n_blocks, body, 0)


def swa_branch(u_b, sink, norm_g):
    bsz, seq, _ = u_b.shape
    kvw = SWA_KV_HEADS * SWA_HEAD_DIM
    return pl.pallas_call(
        _swa_kernel,
        grid=(bsz,),
        in_specs=[pl.BlockSpec(memory_space=pltpu.SMEM),
                  pl.BlockSpec((1, seq, SWA_WIDTH), lambda b: (b, 0, 0)),
                  pl.BlockSpec((1, seq, kvw), lambda b: (b, 0, SWA_WIDTH // kvw)),
                  pl.BlockSpec((1, seq, kvw), lambda b: (b, 0, SWA_WIDTH // kvw + 1)),
                  pl.BlockSpec((1, SWA_WIDTH), lambda b: (0, 0))],
        out_specs=pl.BlockSpec((1, seq, SWA_WIDTH), lambda b: (b, 0, 0)),
        out_shape=jax.ShapeDtypeStruct((bsz, seq, SWA_WIDTH), BF16),
        compiler_params=_cparams(("parallel",)),
        name="swa",
    )(sink, u_b, u_b, u_b, norm_g.reshape(1, -1))


@functools.lru_cache(maxsize=None)
def _hyena_tables(seq):
    t = np.linspace(0.0, 1.0, seq, dtype=np.float32)[:, None]
    w = (np.float32(2.0 * math.pi) * np.arange(seq, dtype=np.float32)[:, None] / np.float32(seq)).astype(np.float32)
    bands = np.linspace(1e-4, HY_BANDS - 1, HY_BANDS, dtype=np.float32)
    arg = (bands * w).astype(np.float32).astype(np.float64)
    z = np.concatenate([t.astype(np.float64), np.cos(arg), -np.sin(arg)], axis=-1)
    z_pad = np.zeros((seq, LANES), np.float32)
    z_pad[:, :HY_EMB] = z.astype(np.float32)
    max_decay = math.log(HY_TARGET) / HY_FAST_PCT
    min_decay = math.log(HY_TARGET) / HY_SLOW_PCT
    rates = np.abs(np.linspace(min_decay, max_decay, HY_WIDTH, dtype=np.float32))[None, :]
    n = 2 * seq - 1
    fj = (np.arange(seq, dtype=np.int64)[:, None] * np.arange(seq, dtype=np.int64)[None, :]) % n
    ang = fj.astype(np.float64) * (2.0 * math.pi / n)
    return z_pad, rates.astype(np.float32), np.cos(ang).astype(np.float32), np.sin(ang).astype(np.float32)


def _hy_filter_kernel(z_ref, rate_ref, w1_ref, b1_ref, fr_ref, w2_ref, b2_ref, w3_ref, hs_ref, hd_ref):
    z = z_ref[...]
    h = jnp.sin(fr_ref[0, 0:1, :] * (_dot(z, w1_ref[0], precision=HIGHEST) + b1_ref[0]))
    h = jnp.sin(fr_ref[0, 1:2, :] * (_dot(h, w2_ref[0], precision=HIGHEST) + b2_ref[0]))
    h = _dot(h, w3_ref[0], precision=HIGHEST)
    window = jnp.exp(-z[:, 0:1] * rate_ref[...])
    h_fwd = h[:, :HY_WIDTH] * window
    h_bwd = h[:, HY_WIDTH:] * window
    first = lax.broadcasted_iota(jnp.int32, h_bwd.shape, 0) == 0
    h_bwd = jnp.where(first, 0.0, h_bwd)
    hs_ref[0] = h_fwd + h_bwd
    hd_ref[0] = h_bwd - h_fwd


def hyena_filters(z, rates, w1p, b1, freq, w2, b2, w3):
    depth = w1p.shape[0]
    seq = z.shape[0]

    def per_layer(*shape):
        return pl.BlockSpec((1,) + shape, lambda l: (l,) + (0,) * len(shape))

    out = jax.ShapeDtypeStruct((depth, seq, HY_WIDTH), F32)
    return pl.pallas_call(
        _hy_filter_kernel,
        grid=(depth,),
        in_specs=[pl.BlockSpec((seq, LANES), lambda l: (0, 0)), pl.BlockSpec((1, HY_WIDTH), lambda l: (0, 0)),
                  per_layer(LANES, HY_FFN), per_layer(1, HY_FFN), per_layer(2, HY_FFN),
                  per_layer(HY_FFN, HY_FFN), per_layer(1, HY_FFN), per_layer(HY_FFN, 2 * HY_WIDTH)],
        out_specs=[per_layer(seq, HY_WIDTH), per_layer(seq, HY_WIDTH)],
        out_shape=[out, out],
        compiler_params=_cparams(("parallel",)),
        name="hyena_filters",
    )(z, rates, w1p, b1, freq, w2, b2, w3)


def _split_bf16(x):
    hi = x.astype(BF16)
    return hi, (x - hi.astype(F32)).astype(BF16)


def _hy_spectrum_kernel(cos_ref, sin_ref, hs_ref, hd_ref, kc_ref, ks_ref):
    seq = cos_ref.shape[0]
    n = 2 * seq - 1
    f = lax.broadcasted_iota(jnp.int32, (seq, 1), 0)
    weight = jnp.where(f == 0, 1.0 / n, 2.0 / n)
    s_hi, s_lo = _split_bf16(hs_ref[0])
    d_hi, d_lo = _split_bf16(hd_ref[0])
    kc_ref[0] = (_dot(cos_ref[...], s_hi) + _dot(cos_ref[...], s_lo)) * weight
    ks_ref[0] = (_dot(sin_ref[...], d_hi) + _dot(sin_ref[...], d_lo)) * weight


def hyena_spectrum(cos_t, sin_t, h_sum, h_diff):
    depth, seq, _ = h_sum.shape
    half = HY_WIDTH // 2
    table = _resident((seq, seq), lambda l, c: (0, 0))
    blk = pl.BlockSpec((1, seq, half), lambda l, c: (l, 0, c))
    out = jax.ShapeDtypeStruct((depth, seq, HY_WIDTH), F32)
    return pl.pallas_call(
        _hy_spectrum_kernel,
        grid=(depth, 2),
        in_specs=[table, table, blk, blk],
        out_specs=[blk, blk],
        out_shape=[out, out],
        compiler_params=_cparams(("parallel", "parallel")),
        name="hyena_spectrum",
    )(cos_t, sin_t, h_sum, h_diff)


def _short_conv(u, w, b):
    seq = u.shape[0]
    row = lax.broadcasted_iota(jnp.int32, u.shape, 0)
    prev = jnp.where(row == 0, 0.0, pltpu.roll(u, 1, axis=0))
    nxt = jnp.where(row == seq - 1, 0.0, pltpu.roll(u, seq - 1, axis=0))
    return prev * w[0:1, :] + u * w[1:2, :] + nxt * w[2:3, :] + b


def _hyena_kernel(x0_ref, x1_ref, v_ref, w0_ref, w1_ref, wv_ref, b0_ref, b1_ref, bv_ref,
                  cos_ref, sin_ref, kc_ref, ks_ref, skip_ref, ng_ref, o_ref, acc):
    c = pl.program_id(1)
    half = x0_ref.shape[2]
    x0 = _short_conv(x0_ref[0], w0_ref[...], b0_ref[...])
    x1 = _short_conv(x1_ref[0], w1_ref[...], b1_ref[...])
    v = _short_conv(v_ref[0], wv_ref[...], bv_ref[...])
    g = x1 * v
    gb = g.astype(BF16)
    uc = _dot(cos_ref[...], gb)
    us = _dot(sin_ref[...], gb)
    kc = kc_ref[0]
    ks = ks_ref[0]
    pr = (uc * kc + us * ks).astype(BF16)
    pi = (uc * ks - us * kc).astype(BF16)
    y = _dot(cos_ref[...], pr) - _dot(sin_ref[...], pi) + g * skip_ref[...]
    y = x0 * y

    @pl.when(c == 0)
    def _():
        acc[:, 0:half] = y

    @pl.when(c == 1)
    def _():
        acc[:, half:2 * half] = y
        full = acc[...]
        o_ref[0] = (full * lax.rsqrt(jnp.mean(full * full, axis=-1, keepdims=True) + RMS_EPS)
                    * ng_ref[...]).astype(o_ref.dtype)


def hyena_branch(u_c, conv_w, conv_b, cos_t, sin_t, k_cos, k_sin, layer, skip, norm_g):
    bsz, seq, _ = u_c.shape
    half = HY_WIDTH // 2

    def stream(first_block):
        return pl.BlockSpec((1, seq, half), lambda b, c: (b, 0, first_block + c))

    def wcol(rows, first_block):
        return pl.BlockSpec((rows, half), lambda b, c: (0, first_block + c))

    table = _resident((seq, seq), lambda b, c: (0, 0))
    spec = pl.BlockSpec((1, seq, half), lambda b, c: (layer, 0, c))
    return pl.pallas_call(
        _hyena_kernel,
        grid=(bsz, 2),
        in_specs=[stream(0), stream(2), stream(4), wcol(3, 0), wcol(3, 2), wcol(3, 4),
                  wcol(1, 0), wcol(1, 2), wcol(1, 4), table, table, spec, spec, wcol(1, 0),
                  pl.BlockSpec((1, HY_WIDTH), lambda b, c: (0, 0))],
        out_specs=pl.BlockSpec((1, seq, HY_WIDTH), lambda b, c: (b, 0, 0)),
        out_shape=jax.ShapeDtypeStruct((bsz, seq, HY_WIDTH), BF16),
        scratch_shapes=[pltpu.VMEM((seq, HY_WIDTH), F32)],
        compiler_params=_cparams(("parallel", "arbitrary")),
        name="hyena",
    )(u_c, u_c, u_c, conv_w, conv_w, conv_w, conv_b, conv_b, conv_b, cos_t, sin_t, k_cos, k_sin,
      skip.reshape(1, -1), norm_g.reshape(1, -1))


def _outproj_kernel(alpha, ya_ref, yb_ref, yc_ref, h_ref, w_ref, b_ref, g_ref, be_ref, wr_ref,
                    h1_ref, h1b_ref, lg_ref):
    na = ya_ref.shape[1]
    nb = yb_ref.shape[1]
    mix = (_dot(ya_ref[...], w_ref[0:na, :]) + _dot(yb_ref[...], w_ref[na:na + nb, :])
           + _dot(yc_ref[...], w_ref[na + nb:, :]) + b_ref[...])
    h1 = _ln(alpha * h_ref[...] + mix, g_ref[...], be_ref[...])
    h1_ref[...] = h1
    h1b = h1.astype(BF16)
    h1b_ref[...] = h1b
    lg_ref[...] = _dot(h1b, wr_ref[...])


def out_projection(alpha, y_a, y_b, y_c, h, w, b, g, be, w_router):
    t, d = h.shape

    def row(width):
        return pl.BlockSpec((ROW_TILE, width), lambda i: (i, 0))

    def whole(r, c):
        return _resident((r, c), lambda i: (0, 0))

    return pl.pallas_call(
        functools.partial(_outproj_kernel, alpha),
        grid=(t // ROW_TILE,),
        in_specs=[row(y_a.shape[1]), row(y_b.shape[1]), row(y_c.shape[1]), row(d), whole(d, d),
                  whole(1, d), whole(1, d), whole(1, d), whole(d, LANES)],
        out_specs=[row(d), row(d), row(LANES)],
        out_shape=[jax.ShapeDtypeStruct((t, d), F32), jax.ShapeDtypeStruct((t, d), BF16),
                   jax.ShapeDtypeStruct((t, LANES), F32)],
        compiler_params=_cparams(("parallel",)),
        name="out_proj_ln",
    )(y_a, y_b, y_c, h, w, b.reshape(1, d), g.reshape(1, d), be.reshape(1, d), w_router)


def _first_index_of_max(x, valid, lane):
    m = jnp.max(jnp.where(valid, x, NEG), axis=-1, keepdims=True)
    idx = jnp.min(jnp.where(valid & (x == m), lane, float(LANES)), axis=-1, keepdims=True)
    return m, idx


def _route_kernel(lg_ref, b_ref, o_ref):
    x = lg_ref[...] + b_ref[...]
    lane = lax.broadcasted_iota(jnp.int32, x.shape, 1).astype(F32)
    is_group = lane < N_GROUPS
    gm, g_idx = _first_index_of_max(x, is_group, lane)
    g_top = 1.0 / jnp.sum(jnp.where(is_group, jnp.exp(x - gm), 0.0), axis=-1, keepdims=True)
    lo = N_GROUPS + g_idx * EXPERTS_PER_GROUP
    in_group = (lane >= lo) & (lane < lo + EXPERTS_PER_GROUP)
    m1, i1 = _first_index_of_max(x, in_group, lane)
    m2, i2 = _first_index_of_max(x, in_group & (lane != i1), lane)
    e2 = jnp.exp(m2 - m1)
    w1 = g_top / (1.0 + e2)
    w2 = g_top * e2 / (1.0 + e2)
    out = jnp.where(lane == 0, i1 - N_GROUPS,
                    jnp.where(lane == 1, i2 - N_GROUPS,
                              jnp.where(lane == 2, w1, jnp.where(lane == 3, w2, 0.0))))
    o_ref[...] = out


def route(logits, bias):
    t = logits.shape[0]
    row = pl.BlockSpec((ROW_TILE, LANES), lambda i: (i, 0))
    return pl.pallas_call(
        _route_kernel,
        grid=(t // ROW_TILE,),
        in_specs=[row, pl.BlockSpec((1, LANES), lambda i: (0, 0))],
        out_specs=row,
        out_shape=jax.ShapeDtypeStruct((t, LANES), F32),
        compiler_params=_cparams(("parallel",)),
        name="route",
    )(logits, bias)


def dispatch_plan(expert_ids, n_steps):
    t = expert_ids.shape[0]
    flat = expert_ids.reshape(-1)
    onehot = (flat[:, None] == jnp.arange(N_EXPERTS, dtype=jnp.int32)[None, :]).astype(jnp.int32)
    counts = jnp.sum(onehot, axis=0)
    rank = jnp.sum((jnp.cumsum(onehot, axis=0) - onehot) * onehot, axis=1)
    padded = (counts + MOE_TILE - 1) // MOE_TILE * MOE_TILE
    ends = jnp.cumsum(padded)
    starts = ends - padded
    pos = starts[flat] + rank
    spare = 2 * t + jnp.arange((n_steps + 1) * MOE_TILE, dtype=jnp.int32) % MOE_TILE
    row_assign = spare.at[pos + MOE_TILE].set(jnp.arange(2 * t, dtype=jnp.int32))
    plane = t + MOE_TILE // 2
    row_token = jnp.minimum(row_assign >> 1, t - 1)
    row_dest = (row_assign >> 1) + (row_assign & 1) * plane
    n_used = (ends[-1] // MOE_TILE).astype(jnp.int32)
    tile_start = jnp.minimum(jnp.arange(n_steps, dtype=jnp.int32), n_used - 1) * MOE_TILE
    tile_expert = jnp.sum((tile_start[:, None] >= ends[None, :]).astype(jnp.int32), axis=1)
    return row_token, row_dest, tile_expert.astype(jnp.int32), n_used.reshape(1)


def _moe_kernel(te_ref, nu_ref, rt_ref, rd_ref, x_hbm, wg_ref, wu_ref, wd_ref, y_hbm, xbuf, ybuf, gsem, ssem):
    i = pl.program_id(0)
    n_used = nu_ref[0]
    tm = xbuf.shape[1]

    def gather_copy(tile, slot, r):
        tok = rt_ref[(tile + 1) * tm + r]
        return pltpu.make_async_copy(x_hbm.at[pl.ds(tok, 1)], xbuf.at[slot, pl.ds(r, 1)], gsem.at[slot])

    def scatter_copy(tile, slot, r):
        dst = rd_ref[(tile + 1) * tm + r]
        return pltpu.make_async_copy(ybuf.at[slot, pl.ds(r, 1)], y_hbm.at[pl.ds(dst, 1)], ssem.at[slot])

    def wait_gather(slot):
        pltpu.make_async_copy(x_hbm.at[pl.ds(0, tm)], xbuf.at[slot], gsem.at[slot]).wait()

    def wait_scatter(slot):
        pltpu.make_async_copy(ybuf.at[slot], y_hbm.at[pl.ds(0, tm)], ssem.at[slot]).wait()

    def looped(copy_fn, tile, slot):
        def body(r, carry):
            copy_fn(tile, slot, r).start()
            return carry
        lax.fori_loop(0, tm, body, 0)

    @pl.when(i == 0)
    def _():
        ybuf[...] = jnp.zeros(ybuf.shape, ybuf.dtype)
        looped(gather_copy, 0, 0)

    for cur in range(2):
        nxt = 1 - cur

        @pl.when((i % 2 == cur) & (i <= n_used))
        def _():
            wait_gather(cur)

        @pl.when((i % 2 == cur) & (i >= 1) & (i <= n_used))
        def _():
            wait_scatter(cur)

        @pl.when((i % 2 == cur) & (i < n_used))
        def _():
            x = xbuf[cur].astype(BF16)
            quarter = tm // 4
            for r in range(0, quarter):
                gather_copy(i + 1, nxt, r).start()
            for r in range(0, quarter):
                scatter_copy(i - 1, nxt, r).start()
            a = _dot(x, wg_ref[0])
            for r in range(quarter, 2 * quarter):
                gather_copy(i + 1, nxt, r).start()
            for r in range(quarter, 2 * quarter):
                scatter_copy(i - 1, nxt, r).start()
            u = _dot(x, wu_ref[0])
            for r in range(2 * quarter, 3 * quarter):
                gather_copy(i + 1, nxt, r).start()
            for r in range(2 * quarter, 3 * quarter):
                scatter_copy(i - 1, nxt, r).start()
            hmid = (a * (1.0 / (1.0 + jnp.exp(-a))) * u).astype(BF16)
            for r in range(3 * quarter, tm):
                gather_copy(i + 1, nxt, r).start()
            for r in range(3 * quarter, tm):
                scatter_copy(i - 1, nxt, r).start()
            ybuf[cur] = _dot(hmid, wd_ref[0])

        @pl.when((i % 2 == cur) & (i == n_used))
        def _():
            looped(scatter_copy, i - 1, nxt)
            wait_scatter(nxt)


def expert_mlps(x, w_gate, w_up, w_down, tile_expert, n_used, row_token, row_dest):
    t, d = x.shape
    n_steps = tile_expert.shape[0]
    f = w_gate.shape[2]
    grid_spec = pltpu.PrefetchScalarGridSpec(
        num_scalar_prefetch=4,
        grid=(n_steps,),
        in_specs=[pl.BlockSpec(memory_space=pl.ANY),
                  pl.BlockSpec((1, d, f), lambda i, te, nu, rt, rd: (te[i], 0, 0)),
                  pl.BlockSpec((1, d, f), lambda i, te, nu, rt, rd: (te[i], 0, 0)),
                  pl.BlockSpec((1, f, d), lambda i, te, nu, rt, rd: (te[i], 0, 0))],
        out_specs=pl.BlockSpec(memory_space=pl.ANY),
        scratch_shapes=[pltpu.VMEM((2, MOE_TILE, d), F32), pltpu.VMEM((2, MOE_TILE, d), F32),
                        pltpu.SemaphoreType.DMA((2,)), pltpu.SemaphoreType.DMA((2,))],
    )
    return pl.pallas_call(
        _moe_kernel,
        grid_spec=grid_spec,
        out_shape=jax.ShapeDtypeStruct((2 * (t + MOE_TILE // 2), d), F32),
        compiler_params=_cparams(("arbitrary",)),
        name="moe_experts",
    )(tile_expert, n_used, row_token, row_dest, x, w_gate, w_up, w_down).reshape(2, t + MOE_TILE // 2, d)


def _combine_kernel(alpha, y1_ref, y2_ref, h_ref, rw_ref, g_ref, b_ref, o_ref, ob_ref):
    rw = rw_ref[...]
    moe = y1_ref[0] * rw[:, 2:3] + y2_ref[0] * rw[:, 3:4]
    y = _ln(alpha * h_ref[...] + moe, g_ref[...], b_ref[...])
    o_ref[...] = y
    ob_ref[...] = y.astype(BF16)


def combine_ln(alpha, y_assign, h, route_out, g, b):
    t, d = h.shape
    tm = ROW_TILE
    row = pl.BlockSpec((tm, d), lambda i: (i, 0))
    vec = pl.BlockSpec((1, d), lambda i: (0, 0))
    return pl.pallas_call(
        functools.partial(_combine_kernel, alpha),
        grid=(t // tm,),
        in_specs=[pl.BlockSpec((1, tm, d), lambda i: (0, i, 0)), pl.BlockSpec((1, tm, d), lambda i: (1, i, 0)), row,
                  pl.BlockSpec((tm, LANES), lambda i: (i, 0)), vec, vec],
        out_specs=[row, row],
        out_shape=[jax.ShapeDtypeStruct((t, d), F32), jax.ShapeDtypeStruct((t, d), BF16)],
        compiler_params=_cparams(("parallel",)),
        name="moe_combine_ln",
    )(y_assign, y_assign, h, route_out, g.reshape(1, d), b.reshape(1, d))


def kernel(x, meta, emb_ln_g, emb_ln_b, w_in, b_in, gla_w2_f, gla_b_f, gla_w2_b, gla_b_b, gla_norm_g,
           swa_sink, swa_norm_g, hy_conv_w, hy_conv_b, hy_w1, hy_b1, hy_freq, hy_w2, hy_b2, hy_w3, hy_skip,
           hy_norm_g, w_out, b_out, ln1_g, ln1_b, router_wg, router_bg, router_we, router_be,
           exp_w_gate, exp_w_up, exp_w_down, ln2_g, ln2_b):
    bsz, seq_in, d = x.shape
    depth = w_in.shape[0]
    seq = seq_in + N_META
    t = bsz * seq
    alpha = (2.0 * depth) ** 0.25

    o_gates, o_r, o_swa, o_hy = 1024, 1056, 1568, 3104
    pad_cols = A_COLS - (o_swa)
    w_a = jnp.concatenate([w_in[:, :, :o_gates], w_in[:, :, o_r:o_swa], w_in[:, :, o_gates:o_r],
                           jnp.zeros((depth, d, pad_cols), w_in.dtype)], axis=-1).astype(BF16)
    b_a = jnp.concatenate([b_in[:, :o_gates], b_in[:, o_r:o_swa], b_in[:, o_gates:o_r],
                           jnp.zeros((depth, pad_cols), b_in.dtype)], axis=-1)
    w_b = w_in[:, :, o_swa:o_hy].astype(BF16)
    w_c = w_in[:, :, o_hy:].astype(BF16)
    w_out_b = w_out.astype(BF16)
    w2f = jnp.zeros((depth, LANES, gla_w2_f.shape[2]), F32).at[:, :GLA_RANK].set(gla_w2_f)
    w2b = jnp.zeros((depth, LANES, gla_w2_b.shape[2]), F32).at[:, GLA_RANK:2 * GLA_RANK].set(gla_w2_b)
    n_r = N_GROUPS + N_EXPERTS
    w_router = jnp.zeros((depth, d, LANES), F32).at[:, :, :N_GROUPS].set(router_wg)
    w_router = w_router.at[:, :, N_GROUPS:n_r].set(router_we).astype(BF16)
    b_router = jnp.zeros((depth, 1, LANES), F32).at[:, 0, :N_GROUPS].set(router_bg)
    b_router = b_router.at[:, 0, N_GROUPS:n_r].set(router_be)
    wg_b = exp_w_gate.astype(BF16)
    wu_b = exp_w_up.astype(BF16)
    wd_b = exp_w_down.astype(BF16)

    z_np, rates_np, cos_np, sin_np = _hyena_tables(seq)
    cos_t = jnp.asarray(cos_np).astype(BF16)
    sin_t = jnp.asarray(sin_np).astype(BF16)
    w1p = jnp.zeros((depth, LANES, HY_FFN), F32).at[:, :HY_EMB].set(hy_w1)
    h_sum, h_diff = hyena_filters(jnp.asarray(z_np), jnp.asarray(rates_np), w1p, hy_b1.reshape(depth, 1, HY_FFN),
                                  hy_freq, hy_w2, hy_b2.reshape(depth, 1, HY_FFN), hy_w3)
    k_cos, k_sin = hyena_spectrum(cos_t, sin_t, h_sum, h_diff)

    n_steps = (2 * t + N_EXPERTS * (MOE_TILE - 1)) // MOE_TILE + 2

    tokens = jnp.concatenate([jnp.broadcast_to(meta.astype(x.dtype)[None], (bsz, N_META, d)), x], axis=1)
    h, hb = ln_rows(tokens.reshape(t, d), emb_ln_g, emb_ln_b)
    for l in range(depth):
        u_a = project(hb, w_a[l], b_a[l].reshape(1, -1), F32).reshape(bsz, seq, -1)
        u_b = project(hb, w_b[l], b_in[l, o_swa:o_hy].reshape(1, -1), BF16).reshape(bsz, seq, -1)
        u_c = project(hb, w_c[l], b_in[l, o_hy:].reshape(1, -1), F32).reshape(bsz, seq, -1)
        y_a = gla_branch(u_a, w2f[l], gla_b_f[l], w2b[l], gla_b_b[l], gla_norm_g[l])
        y_b = swa_branch(u_b, swa_sink[l], swa_norm_g[l])
        y_c = hyena_branch(u_c, hy_conv_w[l], hy_conv_b[l].reshape(1, -1), cos_t, sin_t, k_cos, k_sin, l,
                           hy_skip[l], hy_norm_g[l])
        h1, h1b, logits = out_projection(alpha, y_a.reshape(t, -1), y_b.reshape(t, -1), y_c.reshape(t, -1), h,
                                         w_out_b[l], b_out[l], ln1_g[l], ln1_b[l], w_router[l])
        routed = route(logits, b_router[l])
        expert_ids = routed[:, 0:2].astype(jnp.int32)
        row_token, row_dest, tile_expert, n_used = dispatch_plan(expert_ids, n_steps)
        y_assign = expert_mlps(h1, wg_b[l], wu_b[l], wd_b[l], tile_expert, n_used, row_token, row_dest)
        h, hb = combine_ln(alpha, y_assign, h1, routed, ln2_g[l], ln2_b[l])
    return h.reshape(bsz, seq, d)[:, N_META:]
```

```python
import functools
import math

import jax
import jax.numpy as jnp
import numpy as np
from jax import lax
from jax.experimental import pallas as pl
from jax.experimental.pallas import tpu as pltpu

F32 = jnp.float32
BF16 = jnp.bfloat16
HIGHEST = lax.Precision.HIGHEST

D_MODEL = 2048
N_META = 16
GLA_HEADS = 4
GLA_DK = 64
GLA_DV = 128
GLA_WIDTH = GLA_HEADS * GLA_DV
GLA_RANK = 16
GLA_TAU = 16.0
GLA_CHUNK = 64
SWA_HEADS = 8
SWA_KV_HEADS = 2
SWA_GROUP = SWA_HEADS // SWA_KV_HEADS
SWA_HEAD_DIM = 128
SWA_WIDTH = SWA_HEADS * SWA_HEAD_DIM
WINDOW = 128
HY_WIDTH = 512
HY_BANDS = 16
HY_EMB = 2 * HY_BANDS + 1
HY_FFN = 64
HY_FAST_PCT = 0.3
HY_SLOW_PCT = 1.5
HY_TARGET = 1e-2
N_GROUPS = 4
EXPERTS_PER_GROUP = 4
N_EXPERTS = N_GROUPS * EXPERTS_PER_GROUP
D_EXPERT = 1024
LN_EPS = 1e-5
RMS_EPS = 1e-6
NEG = -1e30

LANES = 128
A_COLS = 1664
GATE_COL0 = 1536
ROW_TILE = 384
MOE_TILE = 256
VMEM_LIMIT = 56 * 1024 * 1024


def _cparams(sem):
    return pltpu.CompilerParams(dimension_semantics=sem, vmem_limit_bytes=VMEM_LIMIT)


def _resident(shape, index_map):
    return pl.BlockSpec(shape, index_map, pipeline_mode=pl.Buffered(1))


def _dot(a, b, **kw):
    return jnp.dot(a, b, preferred_element_type=F32, **kw)


def _dot_nt(a, b):
    return lax.dot_general(a, b, (((1,), (1,)), ((), ())), preferred_element_type=F32)


def _ln(x, g, b):
    mu = jnp.mean(x, axis=-1, keepdims=True)
    xc = x - mu
    var = jnp.mean(xc * xc, axis=-1, keepdims=True)
    return xc * lax.rsqrt(var + LN_EPS) * g + b


def _ln_rows_kernel(x_ref, g_ref, b_ref, o_ref, ob_ref):
    y = _ln(x_ref[...], g_ref[...], b_ref[...])
    o_ref[...] = y
    ob_ref[...] = y.astype(BF16)


def ln_rows(x, g, b):
    t, d = x.shape
    row = pl.BlockSpec((ROW_TILE, d), lambda i: (i, 0))
    vec = pl.BlockSpec((1, d), lambda i: (0, 0))
    return pl.pallas_call(
        _ln_rows_kernel,
        grid=(t // ROW_TILE,),
        in_specs=[row, vec, vec],
        out_specs=[row, row],
        out_shape=[jax.ShapeDtypeStruct((t, d), F32), jax.ShapeDtypeStruct((t, d), BF16)],
        compiler_params=_cparams(("parallel",)),
        name="ln_rows",
    )(x, g.reshape(1, d), b.reshape(1, d))


def _proj_kernel(x_ref, w_ref, b_ref, o_ref):
    o_ref[...] = (_dot(x_ref[...], w_ref[0]) + b_ref[...]).astype(o_ref.dtype)


def project(x, w, layer, b, out_dtype):
    t, k = x.shape
    n = w.shape[2]
    return pl.pallas_call(
        _proj_kernel,
        grid=(t // ROW_TILE,),
        in_specs=[pl.BlockSpec((ROW_TILE, k), lambda i: (i, 0)),
                  _resident((1, k, n), lambda i: (layer, 0, 0)),
                  pl.BlockSpec((1, n), lambda i: (0, 0))],
        out_specs=pl.BlockSpec((ROW_TILE, n), lambda i: (i, 0)),
        out_shape=jax.ShapeDtypeStruct((t, n), out_dtype),
        compiler_params=_cparams(("parallel",)),
        name="in_proj",
    )(x, w, b)


def _log_sigmoid(x):
    return jnp.minimum(x, 0.0) - jnp.log(1.0 + jnp.exp(-jnp.abs(x)))


def _chunk_scan(x, reverse):
    c = x.shape[0]
    idx = lax.broadcasted_iota(jnp.int32, x.shape, 0)
    s = 1
    while s < c:
        if reverse:
            x = x + jnp.where(idx < c - s, pltpu.roll(x, c - s, axis=0), 0.0)
        else:
            x = x + jnp.where(idx >= s, pltpu.roll(x, s, axis=0), 0.0)
        s *= 2
    return x


def _gla_kernel(q_ref, k_ref, v_ref, r_ref, gt_ref, w2f_ref, bf_ref, w2b_ref, bb_ref, ng_ref, o_ref,
                qs, ks, vs, lfs, lbs, acc, st):
    seq = q_ref.shape[1]
    seq_pad = qs.shape[0]
    n_chunks = seq_pad // GLA_CHUNK
    tail = seq_pad - seq
    c = GLA_CHUNK

    gates = gt_ref[0]
    log_f = _log_sigmoid(_dot(gates, w2f_ref[...], precision=HIGHEST) + bf_ref[...]) * (1.0 / GLA_TAU)
    log_b = _log_sigmoid(_dot(gates, w2b_ref[...], precision=HIGHEST) + bb_ref[...]) * (1.0 / GLA_TAU)
    for dst, src in ((qs, q_ref[0]), (ks, k_ref[0]), (vs, v_ref[0]), (lfs, log_f), (lbs, log_b)):
        dst[0:seq, :] = src
        dst[seq:seq_pad, :] = jnp.zeros((tail, dst.shape[1]), F32)

    lane = lax.broadcasted_iota(jnp.int32, (c, 2 * GLA_DK), 1)
    row = lax.broadcasted_iota(jnp.int32, (c, c), 0)
    col = lax.broadcasted_iota(jnp.int32, (c, c), 1)
    scale = GLA_DK ** -0.5

    def direction(log_ref, reverse, first):
        st[...] = jnp.zeros(st.shape, F32)
        keep = (col > row) if reverse else (col <= row)

        def body(i, carry):
            n = (n_chunks - 1 - i) if reverse else i
            r0 = pl.multiple_of(n * c, c)
            la = log_ref[pl.ds(r0, c), :]
            cum = _chunk_scan(la, reverse)
            tot = cum[0:1, :] if reverse else cum[c - 1:c, :]
            q = qs[pl.ds(r0, c), :]
            k = ks[pl.ds(r0, c), :]
            q_dec = q * jnp.exp(cum) * scale
            k_inv = (k * jnp.exp(-cum)).astype(BF16)
            k_dec = (k * jnp.exp(tot - cum)).astype(BF16)
            decay = jnp.exp(tot)
            for h in range(2):
                head_lanes = (lane < GLA_DK) if h == 0 else (lane >= GLA_DK)
                qm = jnp.where(head_lanes, q_dec, 0.0).astype(BF16)
                att = jnp.where(keep, _dot_nt(qm, k_inv), 0.0)
                vh = vs[pl.ds(r0, c), h * GLA_DV:(h + 1) * GLA_DV]
                state = st[h]
                o = _dot(att.astype(BF16), vh.astype(BF16)) + _dot_nt(qm, state.astype(BF16))
                st[h] = state * decay + _dot(vh.T.astype(BF16), k_dec)
                if first:
                    acc[pl.ds(r0, c), h * GLA_DV:(h + 1) * GLA_DV] = o
                else:
                    acc[pl.ds(r0, c), h * GLA_DV:(h + 1) * GLA_DV] += o
            return carry

        lax.fori_loop(0, n_chunks, body, 0)

    direction(lfs, False, True)
    direction(lbs, True, False)

    r = r_ref[0]
    gate = r * (1.0 / (1.0 + jnp.exp(-r)))
    for h in range(2):
        sl = slice(h * GLA_DV, (h + 1) * GLA_DV)
        o = acc[0:seq, sl]
        y = o * lax.rsqrt(jnp.mean(o * o, axis=-1, keepdims=True) + RMS_EPS) * ng_ref[:, sl]
        o_ref[0, :, sl] = (y * gate[:, sl]).astype(o_ref.dtype)


def gla_branch(u_a, w2f, b_f, w2b, b_b, norm_g):
    bsz, seq, _ = u_a.shape
    seq_pad = -(-seq // GLA_CHUNK) * GLA_CHUNK
    pair = 2 * GLA_DK

    def col(width, first_block):
        return pl.BlockSpec((1, seq, width), lambda b, p: (b, 0, first_block + p))

    vec128 = pl.BlockSpec((1, pair), lambda b, p: (0, p))
    return pl.pallas_call(
        _gla_kernel,
        grid=(bsz, GLA_HEADS // 2),
        in_specs=[col(pair, 0), col(pair, 2), col(2 * GLA_DV, 2), col(2 * GLA_DV, 4),
                  pl.BlockSpec((1, seq, LANES), lambda b, p: (b, 0, GATE_COL0 // LANES)),
                  pl.BlockSpec((LANES, pair), lambda b, p: (0, p)), vec128,
                  pl.BlockSpec((LANES, pair), lambda b, p: (0, p)), vec128,
                  pl.BlockSpec((1, 2 * GLA_DV), lambda b, p: (0, p))],
        out_specs=pl.BlockSpec((1, seq, 2 * GLA_DV), lambda b, p: (b, 0, p)),
        out_shape=jax.ShapeDtypeStruct((bsz, seq, GLA_WIDTH), BF16),
        scratch_shapes=[pltpu.VMEM((seq_pad, pair), F32), pltpu.VMEM((seq_pad, pair), F32),
                        pltpu.VMEM((seq_pad, 2 * GLA_DV), F32), pltpu.VMEM((seq_pad, pair), F32),
                        pltpu.VMEM((seq_pad, pair), F32), pltpu.VMEM((seq_pad, 2 * GLA_DV), F32),
                        pltpu.VMEM((2, GLA_DV, pair), F32)],
        compiler_params=_cparams(("parallel", "parallel")),
        name="gla",
    )(u_a, u_a, u_a, u_a, u_a, w2f, b_f.reshape(1, -1), w2b, b_b.reshape(1, -1), norm_g.reshape(1, -1))


def _swa_kernel(sink_ref, q_ref, k_ref, v_ref, ng_ref, o_ref):
    seq = q_ref.shape[1]
    blk = WINDOW
    n_blocks = (seq - N_META) // blk
    scale = SWA_HEAD_DIM ** -0.5
    hd = SWA_HEAD_DIM

    def attend(r0, nq, w0, nk):
        qi = lax.broadcasted_iota(jnp.int32, (nq, nk), 0)
        ki = lax.broadcasted_iota(jnp.int32, (nq, nk), 1)
        dist = jnp.abs(qi - ki + (r0 - w0))
        in_band = dist <= WINDOW
        dist = dist.astype(F32)
        qm = lax.broadcasted_iota(jnp.int32, (nq, blk), 0)
        km = lax.broadcasted_iota(jnp.int32, (nq, blk), 1)
        is_meta = km < N_META
        dist_meta = jnp.abs(qm - km + r0).astype(F32)
        outs = []
        for kv in range(SWA_KV_HEADS):
            ks = slice(kv * hd, (kv + 1) * hd)
            k_band = k_ref[0, pl.ds(w0, nk), ks]
            v_band = v_ref[0, pl.ds(w0, nk), ks]
            k_meta = k_ref[0, 0:blk, ks]
            v_meta = v_ref[0, 0:blk, ks]
            for g in range(SWA_GROUP):
                h = kv * SWA_GROUP + g
                slope = 2.0 ** (-8.0 * (h + 1) / SWA_HEADS)
                q = q_ref[0, pl.ds(r0, nq), h * hd:(h + 1) * hd]
                s_band = jnp.where(in_band, _dot_nt(q, k_band) * scale - slope * dist, NEG)
                s_meta = jnp.where(is_meta, _dot_nt(q, k_meta) * scale - slope * dist_meta, NEG)
                sink = sink_ref[h]
                m = jnp.maximum(jnp.maximum(jnp.max(s_band, axis=-1, keepdims=True),
                                            jnp.max(s_meta, axis=-1, keepdims=True)), sink)
                p_band = jnp.exp(s_band - m)
                p_meta = jnp.exp(s_meta - m)
                den = (jnp.sum(p_band, axis=-1, keepdims=True) + jnp.sum(p_meta, axis=-1, keepdims=True)
                       + jnp.exp(sink - m))
                o = _dot(p_band.astype(BF16), v_band) + _dot(p_meta.astype(BF16), v_meta)
                outs.append(o / den)
        out = jnp.concatenate(outs, axis=1)
        y = out * lax.rsqrt(jnp.mean(out * out, axis=-1, keepdims=True) + RMS_EPS) * ng_ref[...]
        o_ref[0, pl.ds(r0, nq), :] = y.astype(o_ref.dtype)

    attend(0, N_META, N_META, blk)

    last_w0 = N_META + (n_blocks - 3) * blk

    def body(j, carry):
        r0 = pl.multiple_of(N_META + j * blk, N_META)
        w0 = pl.multiple_of(jnp.clip(r0 - blk, N_META, last_w0), N_META)
        attend(r0, blk, w0, 3 * blk)
        return carry

    lax.fori_loop(0, n_blocks, body, 0)


def swa_branch(u_b, sink, norm_g):
    bsz, seq, _ = u_b.shape
    kvw = SWA_KV_HEADS * SWA_HEAD_DIM
    return pl.pallas_call(
        _swa_kernel,
        grid=(bsz,),
        in_specs=[pl.BlockSpec(memory_space=pltpu.SMEM),
                  pl.BlockSpec((1, seq, SWA_WIDTH), lambda b: (b, 0, 0)),
                  pl.BlockSpec((1, seq, kvw), lambda b: (b, 0, SWA_WIDTH // kvw)),
                  pl.BlockSpec((1, seq, kvw), lambda b: (b, 0, SWA_WIDTH // kvw + 1)),
                  pl.BlockSpec((1, SWA_WIDTH), lambda b: (0, 0))],
        out_specs=pl.BlockSpec((1, seq, SWA_WIDTH), lambda b: (b, 0, 0)),
        out_shape=jax.ShapeDtypeStruct((bsz, seq, SWA_WIDTH), BF16),
        compiler_params=_cparams(("parallel",)),
        name="swa",
    )(sink, u_b, u_b, u_b, norm_g.reshape(1, -1))


@functools.lru_cache(maxsize=None)
def _hyena_tables(seq):
    t = np.linspace(0.0, 1.0, seq, dtype=np.float32)[:, None]
    w = (np.float32(2.0 * math.pi) * np.arange(seq, dtype=np.float32)[:, None] / np.float32(seq)).astype(np.float32)
    bands = np.linspace(1e-4, HY_BANDS - 1, HY_BANDS, dtype=np.float32)
    arg = (bands * w).astype(np.float32).astype(np.float64)
    z = np.concatenate([t.astype(np.float64), np.cos(arg), -np.sin(arg)], axis=-1)
    z_pad = np.zeros((seq, LANES), np.float32)
    z_pad[:, :HY_EMB] = z.astype(np.float32)
    max_decay = math.log(HY_TARGET) / HY_FAST_PCT
    min_decay = math.log(HY_TARGET) / HY_SLOW_PCT
    rates = np.abs(np.linspace(min_decay, max_decay, HY_WIDTH, dtype=np.float32))[None, :]
    n = 2 * seq - 1
    fj = (np.arange(seq, dtype=np.int64)[:, None] * np.arange(seq, dtype=np.int64)[None, :]) % n
    ang = fj.astype(np.float64) * (2.0 * math.pi / n)
    return z_pad, rates.astype(np.float32), np.cos(ang).astype(np.float32), np.sin(ang).astype(np.float32)


def _hy_filter_kernel(z_ref, rate_ref, w1_ref, b1_ref, fr_ref, w2_ref, b2_ref, w3_ref, hs_ref, hd_ref):
    z = z_ref[...]
    h = jnp.sin(fr_ref[0, 0:1, :] * (_dot(z, w1_ref[0], precision=HIGHEST) + b1_ref[0]))
    h = jnp.sin(fr_ref[0, 1:2, :] * (_dot(h, w2_ref[0], precision=HIGHEST) + b2_ref[0]))
    h = _dot(h, w3_ref[0], precision=HIGHEST)
    window = jnp.exp(-z[:, 0:1] * rate_ref[...])
    h_fwd = h[:, :HY_WIDTH] * window
    h_bwd = h[:, HY_WIDTH:] * window
    first = lax.broadcasted_iota(jnp.int32, h_bwd.shape, 0) == 0
    h_bwd = jnp.where(first, 0.0, h_bwd)
    hs_ref[0] = h_fwd + h_bwd
    hd_ref[0] = h_bwd - h_fwd


def hyena_filters(z, rates, w1p, b1, freq, w2, b2, w3):
    depth = w1p.shape[0]
    seq = z.shape[0]

    def per_layer(*shape):
        return pl.BlockSpec((1,) + shape, lambda l: (l,) + (0,) * len(shape))

    out = jax.ShapeDtypeStruct((depth, seq, HY_WIDTH), F32)
    return pl.pallas_call(
        _hy_filter_kernel,
        grid=(depth,),
        in_specs=[pl.BlockSpec((seq, LANES), lambda l: (0, 0)), pl.BlockSpec((1, HY_WIDTH), lambda l: (0, 0)),
                  per_layer(LANES, HY_FFN), per_layer(1, HY_FFN), per_layer(2, HY_FFN),
                  per_layer(HY_FFN, HY_FFN), per_layer(1, HY_FFN), per_layer(HY_FFN, 2 * HY_WIDTH)],
        out_specs=[per_layer(seq, HY_WIDTH), per_layer(seq, HY_WIDTH)],
        out_shape=[out, out],
        compiler_params=_cparams(("parallel",)),
        name="hyena_filters",
    )(z, rates, w1p, b1, freq, w2, b2, w3)


def _split_bf16(x):
    hi = x.astype(BF16)
    return hi, (x - hi.astype(F32)).astype(BF16)


def _hy_spectrum_kernel(cos_ref, sin_ref, hs_ref, hd_ref, kc_ref, ks_ref):
    seq = cos_ref.shape[0]
    n = 2 * seq - 1
    f = lax.broadcasted_iota(jnp.int32, (seq, 1), 0)
    weight = jnp.where(f == 0, 1.0 / n, 2.0 / n)
    s_hi, s_lo = _split_bf16(hs_ref[0])
    d_hi, d_lo = _split_bf16(hd_ref[0])
    kc_ref[0] = (_dot(cos_ref[...], s_hi) + _dot(cos_ref[...], s_lo)) * weight
    ks_ref[0] = (_dot(sin_ref[...], d_hi) + _dot(sin_ref[...], d_lo)) * weight


def hyena_spectrum(cos_t, sin_t, h_sum, h_diff):
    depth, seq, _ = h_sum.shape
    half = HY_WIDTH // 2
    table = _resident((seq, seq), lambda l, c: (0, 0))
    blk = pl.BlockSpec((1, seq, half), lambda l, c: (l, 0, c))
    out = jax.ShapeDtypeStruct((depth, seq, HY_WIDTH), F32)
    return pl.pallas_call(
        _hy_spectrum_kernel,
        grid=(depth, 2),
        in_specs=[table, table, blk, blk],
        out_specs=[blk, blk],
        out_shape=[out, out],
        compiler_params=_cparams(("parallel", "parallel")),
        name="hyena_spectrum",
    )(cos_t, sin_t, h_sum, h_diff)


def _short_conv(u, w, b):
    seq = u.shape[0]
    row = lax.broadcasted_iota(jnp.int32, u.shape, 0)
    prev = jnp.where(row == 0, 0.0, pltpu.roll(u, 1, axis=0))
    nxt = jnp.where(row == seq - 1, 0.0, pltpu.roll(u, seq - 1, axis=0))
    return prev * w[0:1, :] + u * w[1:2, :] + nxt * w[2:3, :] + b


def _hyena_kernel(x0_ref, x1_ref, v_ref, w0_ref, w1_ref, wv_ref, b0_ref, b1_ref, bv_ref,
                  cos_ref, sin_ref, kc_ref, ks_ref, skip_ref, ng_ref, o_ref, acc):
    c = pl.program_id(1)
    half = x0_ref.shape[2]
    x0 = _short_conv(x0_ref[0], w0_ref[...], b0_ref[...])
    x1 = _short_conv(x1_ref[0], w1_ref[...], b1_ref[...])
    v = _short_conv(v_ref[0], wv_ref[...], bv_ref[...])
    g = x1 * v
    gb = g.astype(BF16)
    uc = _dot(cos_ref[...], gb)
    us = _dot(sin_ref[...], gb)
    kc = kc_ref[0]
    ks = ks_ref[0]
    pr = (uc * kc + us * ks).astype(BF16)
    pi = (uc * ks - us * kc).astype(BF16)
    y = _dot(cos_ref[...], pr) - _dot(sin_ref[...], pi) + g * skip_ref[...]
    y = x0 * y

    @pl.when(c == 0)
    def _():
        acc[:, 0:half] = y

    @pl.when(c == 1)
    def _():
        acc[:, half:2 * half] = y
        full = acc[...]
        o_ref[0] = (full * lax.rsqrt(jnp.mean(full * full, axis=-1, keepdims=True) + RMS_EPS)
                    * ng_ref[...]).astype(o_ref.dtype)


def hyena_branch(u_c, conv_w, conv_b, cos_t, sin_t, k_cos, k_sin, layer, skip, norm_g):
    bsz, seq, _ = u_c.shape
    half = HY_WIDTH // 2

    def stream(first_block):
        return pl.BlockSpec((1, seq, half), lambda b, c: (b, 0, first_block + c))

    def wcol(rows, first_block):
        return pl.BlockSpec((rows, half), lambda b, c: (0, first_block + c))

    table = _resident((seq, seq), lambda b, c: (0, 0))
    spec = pl.BlockSpec((1, seq, half), lambda b, c: (layer, 0, c))
    return pl.pallas_call(
        _hyena_kernel,
        grid=(bsz, 2),
        in_specs=[stream(0), stream(2), stream(4), wcol(3, 0), wcol(3, 2), wcol(3, 4),
                  wcol(1, 0), wcol(1, 2), wcol(1, 4), table, table, spec, spec, wcol(1, 0),
                  pl.BlockSpec((1, HY_WIDTH), lambda b, c: (0, 0))],
        out_specs=pl.BlockSpec((1, seq, HY_WIDTH), lambda b, c: (b, 0, 0)),
        out_shape=jax.ShapeDtypeStruct((bsz, seq, HY_WIDTH), BF16),
        scratch_shapes=[pltpu.VMEM((seq, HY_WIDTH), F32)],
        compiler_params=_cparams(("parallel", "arbitrary")),
        name="hyena",
    )(u_c, u_c, u_c, conv_w, conv_w, conv_w, conv_b, conv_b, conv_b, cos_t, sin_t, k_cos, k_sin,
      skip.reshape(1, -1), norm_g.reshape(1, -1))


def _first_index_of_max(x, valid, lane):
    m = jnp.max(jnp.where(valid, x, NEG), axis=-1, keepdims=True)
    idx = jnp.min(jnp.where(valid & (x == m), lane, float(LANES)), axis=-1, keepdims=True)
    return m, idx


def _route(x):
    lane = lax.broadcasted_iota(jnp.int32, x.shape, 1).astype(F32)
    is_group = lane < N_GROUPS
    gm, g_idx = _first_index_of_max(x, is_group, lane)
    g_top = 1.0 / jnp.sum(jnp.where(is_group, jnp.exp(x - gm), 0.0), axis=-1, keepdims=True)
    lo = N_GROUPS + g_idx * EXPERTS_PER_GROUP
    in_group = (lane >= lo) & (lane < lo + EXPERTS_PER_GROUP)
    m1, i1 = _first_index_of_max(x, in_group, lane)
    m2, i2 = _first_index_of_max(x, in_group & (lane != i1), lane)
    e2 = jnp.exp(m2 - m1)
    w1 = g_top / (1.0 + e2)
    w2 = g_top * e2 / (1.0 + e2)
    return jnp.where(lane == 0, i1 - N_GROUPS,
                     jnp.where(lane == 1, i2 - N_GROUPS,
                               jnp.where(lane == 2, w1, jnp.where(lane == 3, w2, 0.0))))


def _outproj_kernel(alpha, ya_ref, yb_ref, yc_ref, h_ref, w_ref, b_ref, g_ref, be_ref, wr_ref, br_ref,
                    h1_ref, rt_ref):
    na = ya_ref.shape[1]
    nb = yb_ref.shape[1]
    mix = (_dot(ya_ref[...], w_ref[0, 0:na, :]) + _dot(yb_ref[...], w_ref[0, na:na + nb, :])
           + _dot(yc_ref[...], w_ref[0, na + nb:, :]) + b_ref[...])
    h1 = _ln(alpha * h_ref[...] + mix, g_ref[...], be_ref[...])
    h1_ref[...] = h1
    rt_ref[...] = _route(_dot(h1.astype(BF16), wr_ref[0]) + br_ref[0])


def out_projection(alpha, y_a, y_b, y_c, h, w, layer, b, g, be, w_router, b_router):
    t, d = h.shape

    def row(width):
        return pl.BlockSpec((ROW_TILE, width), lambda i: (i, 0))

    def vec():
        return pl.BlockSpec((1, d), lambda i: (0, 0))

    return pl.pallas_call(
        functools.partial(_outproj_kernel, alpha),
        grid=(t // ROW_TILE,),
        in_specs=[row(y_a.shape[1]), row(y_b.shape[1]), row(y_c.shape[1]), row(d),
                  _resident((1, d, d), lambda i: (layer, 0, 0)), vec(), vec(), vec(),
                  _resident((1, d, LANES), lambda i: (layer, 0, 0)),
                  pl.BlockSpec((1, 1, LANES), lambda i: (layer, 0, 0))],
        out_specs=[row(d), row(LANES)],
        out_shape=[jax.ShapeDtypeStruct((t, d), F32), jax.ShapeDtypeStruct((t, LANES), F32)],
        compiler_params=_cparams(("parallel",)),
        name="out_proj_ln",
    )(y_a, y_b, y_c, h, w, b.reshape(1, d), g.reshape(1, d), be.reshape(1, d), w_router, b_router)


def dispatch_plan(expert_ids, n_steps):
    t = expert_ids.shape[0]
    flat = expert_ids.reshape(-1)
    onehot = (flat[:, None] == jnp.arange(N_EXPERTS, dtype=jnp.int32)[None, :]).astype(jnp.int32)
    counts = jnp.sum(onehot, axis=0)
    rank = jnp.sum((jnp.cumsum(onehot, axis=0) - onehot) * onehot, axis=1)
    padded = (counts + MOE_TILE - 1) // MOE_TILE * MOE_TILE
    ends = jnp.cumsum(padded)
    starts = ends - padded
    pos = starts[flat] + rank
    spare = 2 * t + jnp.arange((n_steps + 1) * MOE_TILE, dtype=jnp.int32) % MOE_TILE
    row_assign = spare.at[pos + MOE_TILE].set(jnp.arange(2 * t, dtype=jnp.int32))
    plane = t + MOE_TILE // 2
    row_token = jnp.minimum(row_assign >> 1, t - 1)
    row_dest = (row_assign >> 1) + (row_assign & 1) * plane
    n_used = (ends[-1] // MOE_TILE).astype(jnp.int32)
    tile_start = jnp.minimum(jnp.arange(n_steps, dtype=jnp.int32), n_used - 1) * MOE_TILE
    tile_expert = jnp.sum((tile_start[:, None] >= ends[None, :]).astype(jnp.int32), axis=1)
    return row_token, row_dest, tile_expert.astype(jnp.int32), n_used.reshape(1)


def _moe_kernel(te_ref, nu_ref, rt_ref, rd_ref, x_hbm, wg_ref, wu_ref, wd_ref, y_hbm, xbuf, ybuf, gsem, ssem):
    i = pl.program_id(0)
    n_used = nu_ref[0]
    tm = xbuf.shape[1]

    def gather_copy(tile, slot, r):
        tok = rt_ref[(tile + 1) * tm + r]
        return pltpu.make_async_copy(x_hbm.at[pl.ds(tok, 1)], xbuf.at[slot, pl.ds(r, 1)], gsem.at[slot])

    def scatter_copy(tile, slot, r):
        dst = rd_ref[(tile + 1) * tm + r]
        return pltpu.make_async_copy(ybuf.at[slot, pl.ds(r, 1)], y_hbm.at[pl.ds(dst, 1)], ssem.at[slot])

    def wait_gather(slot):
        pltpu.make_async_copy(x_hbm.at[pl.ds(0, tm)], xbuf.at[slot], gsem.at[slot]).wait()

    def wait_scatter(slot):
        pltpu.make_async_copy(ybuf.at[slot], y_hbm.at[pl.ds(0, tm)], ssem.at[slot]).wait()

    def looped(copy_fn, tile, slot):
        def body(r, carry):
            copy_fn(tile, slot, r).start()
            return carry
        lax.fori_loop(0, tm, body, 0)

    @pl.when(i == 0)
    def _():
        ybuf[...] = jnp.zeros(ybuf.shape, ybuf.dtype)
        looped(gather_copy, 0, 0)

    for cur in range(2):
        nxt = 1 - cur

        @pl.when((i % 2 == cur) & (i <= n_used))
        def _():
            wait_gather(cur)

        @pl.when((i % 2 == cur) & (i >= 1) & (i <= n_used))
        def _():
            wait_scatter(cur)

        @pl.when((i % 2 == cur) & (i < n_used))
        def _():
            x = xbuf[cur].astype(BF16)

            def neighbour_copies(lo, hi):
                for r in range(lo, hi):
                    gather_copy(i + 1, nxt, r).start(priority=r % 2)
                    scatter_copy(i - 1, nxt, r).start(priority=(r + 1) % 2)

            quarter = tm // 4
            neighbour_copies(0, quarter)
            a = _dot(x, wg_ref[0])
            neighbour_copies(quarter, 2 * quarter)
            u = _dot(x, wu_ref[0])
            neighbour_copies(2 * quarter, 3 * quarter)
            hmid = (a * (1.0 / (1.0 + jnp.exp(-a))) * u).astype(BF16)
            neighbour_copies(3 * quarter, tm)
            ybuf[cur] = _dot(hmid, wd_ref[0])

        @pl.when((i % 2 == cur) & (i == n_used))
        def _():
            looped(scatter_copy, i - 1, nxt)
            wait_scatter(nxt)


def expert_mlps(x, w_gate, w_up, w_down, tile_expert, n_used, row_token, row_dest):
    t, d = x.shape
    n_steps = tile_expert.shape[0]
    f = w_gate.shape[2]
    grid_spec = pltpu.PrefetchScalarGridSpec(
        num_scalar_prefetch=4,
        grid=(n_steps,),
        in_specs=[pl.BlockSpec(memory_space=pl.ANY),
                  pl.BlockSpec((1, d, f), lambda i, te, nu, rt, rd: (te[i], 0, 0)),
                  pl.BlockSpec((1, d, f), lambda i, te, nu, rt, rd: (te[i], 0, 0)),
                  pl.BlockSpec((1, f, d), lambda i, te, nu, rt, rd: (te[i], 0, 0))],
        out_specs=pl.BlockSpec(memory_space=pl.ANY),
        scratch_shapes=[pltpu.VMEM((2, MOE_TILE, d), F32), pltpu.VMEM((2, MOE_TILE, d), F32),
                        pltpu.SemaphoreType.DMA((2,)), pltpu.SemaphoreType.DMA((2,))],
    )
    return pl.pallas_call(
        _moe_kernel,
        grid_spec=grid_spec,
        out_shape=jax.ShapeDtypeStruct((2 * (t + MOE_TILE // 2), d), F32),
        compiler_params=_cparams(("arbitrary",)),
        name="moe_experts",
    )(tile_expert, n_used, row_token, row_dest, x, w_gate, w_up, w_down).reshape(2, t + MOE_TILE // 2, d)


def _combine_kernel(alpha, y1_ref, y2_ref, h_ref, rw_ref, g_ref, b_ref, o_ref, ob_ref):
    rw = rw_ref[...]
    moe = y1_ref[0] * rw[:, 2:3] + y2_ref[0] * rw[:, 3:4]
    y = _ln(alpha * h_ref[...] + moe, g_ref[...], b_ref[...])
    o_ref[...] = y
    ob_ref[...] = y.astype(BF16)


def combine_ln(alpha, y_assign, h, route_out, g, b):
    t, d = h.shape
    tm = ROW_TILE
    row = pl.BlockSpec((tm, d), lambda i: (i, 0))
    vec = pl.BlockSpec((1, d), lambda i: (0, 0))
    return pl.pallas_call(
        functools.partial(_combine_kernel, alpha),
        grid=(t // tm,),
        in_specs=[pl.BlockSpec((1, tm, d), lambda i: (0, i, 0)), pl.BlockSpec((1, tm, d), lambda i: (1, i, 0)), row,
                  pl.BlockSpec((tm, LANES), lambda i: (i, 0)), vec, vec],
        out_specs=[row, row],
        out_shape=[jax.ShapeDtypeStruct((t, d), F32), jax.ShapeDtypeStruct((t, d), BF16)],
        compiler_params=_cparams(("parallel",)),
        name="moe_combine_ln",
    )(y_assign, y_assign, h, route_out, g.reshape(1, d), b.reshape(1, d))


def kernel(x, meta, emb_ln_g, emb_ln_b, w_in, b_in, gla_w2_f, gla_b_f, gla_w2_b, gla_b_b, gla_norm_g,
           swa_sink, swa_norm_g, hy_conv_w, hy_conv_b, hy_w1, hy_b1, hy_freq, hy_w2, hy_b2, hy_w3, hy_skip,
           hy_norm_g, w_out, b_out, ln1_g, ln1_b, router_wg, router_bg, router_we, router_be,
           exp_w_gate, exp_w_up, exp_w_down, ln2_g, ln2_b):
    bsz, seq_in, d = x.shape
    depth = w_in.shape[0]
    seq = seq_in + N_META
    t = bsz * seq
    alpha = (2.0 * depth) ** 0.25

    o_gates, o_r, o_swa, o_hy = 1024, 1056, 1568, 3104
    pad_cols = A_COLS - (o_swa)
    w_a = jnp.concatenate([w_in[:, :, :o_gates], w_in[:, :, o_r:o_swa], w_in[:, :, o_gates:o_r],
                           jnp.zeros((depth, d, pad_cols), w_in.dtype)], axis=-1).astype(BF16)
    b_a = jnp.concatenate([b_in[:, :o_gates], b_in[:, o_r:o_swa], b_in[:, o_gates:o_r],
                           jnp.zeros((depth, pad_cols), b_in.dtype)], axis=-1)
    w_b = w_in[:, :, o_swa:o_hy].astype(BF16)
    w_c = w_in[:, :, o_hy:].astype(BF16)
    w_out_b = w_out.astype(BF16)
    w2f = jnp.zeros((depth, LANES, gla_w2_f.shape[2]), F32).at[:, :GLA_RANK].set(gla_w2_f)
    w2b = jnp.zeros((depth, LANES, gla_w2_b.shape[2]), F32).at[:, GLA_RANK:2 * GLA_RANK].set(gla_w2_b)
    n_r = N_GROUPS + N_EXPERTS
    w_router = jnp.zeros((depth, d, LANES), F32).at[:, :, :N_GROUPS].set(router_wg)
    w_router = w_router.at[:, :, N_GROUPS:n_r].set(router_we).astype(BF16)
    b_router = jnp.zeros((depth, 1, LANES), F32).at[:, 0, :N_GROUPS].set(router_bg)
    b_router = b_router.at[:, 0, N_GROUPS:n_r].set(router_be)
    d_exp = exp_w_gate.shape[-1]
    wg_b = exp_w_gate.astype(BF16).reshape(depth * N_EXPERTS, d, d_exp)
    wu_b = exp_w_up.astype(BF16).reshape(depth * N_EXPERTS, d, d_exp)
    wd_b = exp_w_down.astype(BF16).reshape(depth * N_EXPERTS, d_exp, d)

    z_np, rates_np, cos_np, sin_np = _hyena_tables(seq)
    cos_t = jnp.asarray(cos_np).astype(BF16)
    sin_t = jnp.asarray(sin_np).astype(BF16)
    w1p = jnp.zeros((depth, LANES, HY_FFN), F32).at[:, :HY_EMB].set(hy_w1)
    h_sum, h_diff = hyena_filters(jnp.asarray(z_np), jnp.asarray(rates_np), w1p, hy_b1.reshape(depth, 1, HY_FFN),
                                  hy_freq, hy_w2, hy_b2.reshape(depth, 1, HY_FFN), hy_w3)
    k_cos, k_sin = hyena_spectrum(cos_t, sin_t, h_sum, h_diff)

    n_steps = (2 * t + N_EXPERTS * (MOE_TILE - 1)) // MOE_TILE + 2

    tokens = jnp.concatenate([jnp.broadcast_to(meta.astype(x.dtype)[None], (bsz, N_META, d)), x], axis=1)
    h, hb = ln_rows(tokens.reshape(t, d), emb_ln_g, emb_ln_b)
    for l in range(depth):
        u_a = project(hb, w_a, l, b_a[l].reshape(1, -1), F32).reshape(bsz, seq, -1)
        u_b = project(hb, w_b, l, b_in[l, o_swa:o_hy].reshape(1, -1), BF16).reshape(bsz, seq, -1)
        u_c = project(hb, w_c, l, b_in[l, o_hy:].reshape(1, -1), F32).reshape(bsz, seq, -1)
        y_a = gla_branch(u_a, w2f[l], gla_b_f[l], w2b[l], gla_b_b[l], gla_norm_g[l])
        y_b = swa_branch(u_b, swa_sink[l], swa_norm_g[l])
        y_c = hyena_branch(u_c, hy_conv_w[l], hy_conv_b[l].reshape(1, -1), cos_t, sin_t, k_cos, k_sin, l,
                           hy_skip[l], hy_norm_g[l])
        h1, routed = out_projection(alpha, y_a.reshape(t, -1), y_b.reshape(t, -1), y_c.reshape(t, -1), h,
                                    w_out_b, l, b_out[l], ln1_g[l], ln1_b[l], w_router, b_router)
        expert_ids = routed[:, 0:2].astype(jnp.int32)
        row_token, row_dest, tile_expert, n_used = dispatch_plan(expert_ids, n_steps)
        y_assign = expert_mlps(h1, wg_b, wu_b, wd_b, tile_expert + l * N_EXPERTS, n_used, row_token, row_dest)
        h, hb = combine_ln(alpha, y_assign, h1, routed, ln2_g[l], ln2_b[l])
    return h.reshape(bsz, seq, d)[:, N_META:]
```

```python
import functools
import math

import jax
import jax.numpy as jnp
import numpy as np
from jax import lax
from jax.experimental import pallas as pl
from jax.experimental.pallas import tpu as pltpu

F32 = jnp.float32
BF16 = jnp.bfloat16
HIGHEST = lax.Precision.HIGHEST

D_MODEL = 2048
N_META = 16
GLA_HEADS = 4
GLA_DK = 64
GLA_DV = 128
GLA_WIDTH = GLA_HEADS * GLA_DV
GLA_RANK = 16
GLA_TAU = 16.0
GLA_CHUNK = 64
GLA_UNROLL = 11
SWA_HEADS = 8
SWA_KV_HEADS = 2
SWA_GROUP = SWA_HEADS // SWA_KV_HEADS
SWA_HEAD_DIM = 128
SWA_WIDTH = SWA_HEADS * SWA_HEAD_DIM
WINDOW = 128
HY_WIDTH = 512
HY_BANDS = 16
HY_EMB = 2 * HY_BANDS + 1
HY_FFN = 64
HY_FAST_PCT = 0.3
HY_SLOW_PCT = 1.5
HY_TARGET = 1e-2
N_GROUPS = 4
EXPERTS_PER_GROUP = 4
N_EXPERTS = N_GROUPS * EXPERTS_PER_GROUP
D_EXPERT = 1024
LN_EPS = 1e-5
RMS_EPS = 1e-6
NEG = -1e30

LANES = 128
A_COLS = 1664
GATE_COL0 = 1536
ROW_TILE = 384
MOE_TILE = 256
VMEM_LIMIT = 56 * 1024 * 1024


def _cparams(sem):
    return pltpu.CompilerParams(dimension_semantics=sem, vmem_limit_bytes=VMEM_LIMIT)


def _resident(shape, index_map):
    return pl.BlockSpec(shape, index_map, pipeline_mode=pl.Buffered(1))


def _dot(a, b, **kw):
    return jnp.dot(a, b, preferred_element_type=F32, **kw)


def _dot_nt(a, b):
    return lax.dot_general(a, b, (((1,), (1,)), ((), ())), preferred_element_type=F32)


def _ln(x, g, b):
    mu = jnp.mean(x, axis=-1, keepdims=True)
    xc = x - mu
    var = jnp.mean(xc * xc, axis=-1, keepdims=True)
    return xc * lax.rsqrt(var + LN_EPS) * g + b


def _ln_rows_kernel(x_ref, g_ref, b_ref, o_ref, ob_ref):
    y = _ln(x_ref[...], g_ref[...], b_ref[...])
    o_ref[...] = y
    ob_ref[...] = y.astype(BF16)


def ln_rows(x, g, b):
    t, d = x.shape
    row = pl.BlockSpec((ROW_TILE, d), lambda i: (i, 0))
    vec = pl.BlockSpec((1, d), lambda i: (0, 0))
    return pl.pallas_call(
        _ln_rows_kernel,
        grid=(t // ROW_TILE,),
        in_specs=[row, vec, vec],
        out_specs=[row, row],
        out_shape=[jax.ShapeDtypeStruct((t, d), F32), jax.ShapeDtypeStruct((t, d), BF16)],
        compiler_params=_cparams(("parallel",)),
        name="ln_rows",
    )(x, g.reshape(1, d), b.reshape(1, d))


def _proj_kernel(x_ref, w_ref, b_ref, o_ref):
    o_ref[...] = (_dot(x_ref[...], w_ref[0]) + b_ref[...]).astype(o_ref.dtype)


def project(x, w, layer, b, out_dtype):
    t, k = x.shape
    n = w.shape[2]
    return pl.pallas_call(
        _proj_kernel,
        grid=(t // ROW_TILE,),
        in_specs=[pl.BlockSpec((ROW_TILE, k), lambda i: (i, 0)),
                  _resident((1, k, n), lambda i: (layer, 0, 0)),
                  pl.BlockSpec((1, n), lambda i: (0, 0))],
        out_specs=pl.BlockSpec((ROW_TILE, n), lambda i: (i, 0)),
        out_shape=jax.ShapeDtypeStruct((t, n), out_dtype),
        compiler_params=_cparams(("parallel",)),
        name="in_proj",
    )(x, w, b)


def _log_sigmoid(x):
    return jnp.minimum(x, 0.0) - jnp.log(1.0 + jnp.exp(-jnp.abs(x)))


def _chunk_scan(x, reverse):
    c = x.shape[0]
    idx = lax.broadcasted_iota(jnp.int32, x.shape, 0)
    s = 1
    while s < c:
        if reverse:
            x = x + jnp.where(idx < c - s, pltpu.roll(x, c - s, axis=0), 0.0)
        else:
            x = x + jnp.where(idx >= s, pltpu.roll(x, s, axis=0), 0.0)
        s *= 2
    return x


def _gla_kernel(q_ref, k_ref, v_ref, r_ref, gt_ref, w2f_ref, bf_ref, w2b_ref, bb_ref, ng_ref, o_ref,
                qs, ks, vs, lfs, lbs, acc, acc_b, st):
    seq = q_ref.shape[1]
    seq_pad = qs.shape[0]
    n_chunks = seq_pad // GLA_CHUNK
    tail = seq_pad - seq
    c = GLA_CHUNK

    gates = gt_ref[0]
    log_f = _log_sigmoid(_dot(gates, w2f_ref[...], precision=HIGHEST) + bf_ref[...]) * (1.0 / GLA_TAU)
    log_b = _log_sigmoid(_dot(gates, w2b_ref[...], precision=HIGHEST) + bb_ref[...]) * (1.0 / GLA_TAU)
    for dst, src in ((qs, q_ref[0]), (ks, k_ref[0]), (vs, v_ref[0]), (lfs, log_f), (lbs, log_b)):
        dst[0:seq, :] = src
        dst[seq:seq_pad, :] = jnp.zeros((tail, dst.shape[1]), F32)

    lane = lax.broadcasted_iota(jnp.int32, (c, 2 * GLA_DK), 1)
    row = lax.broadcasted_iota(jnp.int32, (c, c), 0)
    col = lax.broadcasted_iota(jnp.int32, (c, c), 1)
    scale = GLA_DK ** -0.5

    st[...] = jnp.zeros(st.shape, F32)

    def chunk(n, log_ref, reverse, out_ref):
        keep = (col > row) if reverse else (col <= row)
        r0 = pl.multiple_of(n * c, c)
        cum = _chunk_scan(log_ref[pl.ds(r0, c), :], reverse)
        tot = cum[0:1, :] if reverse else cum[c - 1:c, :]
        q = qs[pl.ds(r0, c), :]
        k = ks[pl.ds(r0, c), :]
        q_dec = q * jnp.exp(cum) * scale
        k_inv = (k * jnp.exp(-cum)).astype(BF16)
        k_dec = (k * jnp.exp(tot - cum)).astype(BF16)
        decay = jnp.exp(tot)
        for h in range(2):
            slot = 2 * int(reverse) + h
            head_lanes = (lane < GLA_DK) if h == 0 else (lane >= GLA_DK)
            qm = jnp.where(head_lanes, q_dec, 0.0).astype(BF16)
            att = jnp.where(keep, _dot_nt(qm, k_inv), 0.0)
            vh = vs[pl.ds(r0, c), h * GLA_DV:(h + 1) * GLA_DV]
            state = st[slot]
            out_ref[pl.ds(r0, c), h * GLA_DV:(h + 1) * GLA_DV] = (
                _dot(att.astype(BF16), vh.astype(BF16)) + _dot_nt(qm, state.astype(BF16)))
            st[slot] = state * decay + _dot(vh.T.astype(BF16), k_dec)

    def body(i, carry):
        chunk(i, lfs, False, acc)
        chunk(n_chunks - 1 - i, lbs, True, acc_b)
        return carry

    lax.fori_loop(0, n_chunks, body, 0, unroll=GLA_UNROLL)

    r = r_ref[0]
    gate = r * (1.0 / (1.0 + jnp.exp(-r)))
    for h in range(2):
        sl = slice(h * GLA_DV, (h + 1) * GLA_DV)
        o = acc[0:seq, sl] + acc_b[0:seq, sl]
        y = o * lax.rsqrt(jnp.mean(o * o, axis=-1, keepdims=True) + RMS_EPS) * ng_ref[:, sl]
        o_ref[0, :, sl] = (y * gate[:, sl]).astype(o_ref.dtype)


def gla_branch(u_a, w2f, b_f, w2b, b_b, norm_g):
    bsz, seq, _ = u_a.shape
    seq_pad = -(-seq // GLA_CHUNK) * GLA_CHUNK
    pair = 2 * GLA_DK

    def col(width, first_block):
        return pl.BlockSpec((1, seq, width), lambda b, p: (b, 0, first_block + p))

    vec128 = pl.BlockSpec((1, pair), lambda b, p: (0, p))
    return pl.pallas_call(
        _gla_kernel,
        grid=(bsz, GLA_HEADS // 2),
        in_specs=[col(pair, 0), col(pair, 2), col(2 * GLA_DV, 2), col(2 * GLA_DV, 4),
                  pl.BlockSpec((1, seq, LANES), lambda b, p: (b, 0, GATE_COL0 // LANES)),
                  pl.BlockSpec((LANES, pair), lambda b, p: (0, p)), vec128,
                  pl.BlockSpec((LANES, pair), lambda b, p: (0, p)), vec128,
                  pl.BlockSpec((1, 2 * GLA_DV), lambda b, p: (0, p))],
        out_specs=pl.BlockSpec((1, seq, 2 * GLA_DV), lambda b, p: (b, 0, p)),
        out_shape=jax.ShapeDtypeStruct((bsz, seq, GLA_WIDTH), BF16),
        scratch_shapes=[pltpu.VMEM((seq_pad, pair), F32), pltpu.VMEM((seq_pad, pair), F32),
                        pltpu.VMEM((seq_pad, 2 * GLA_DV), F32), pltpu.VMEM((seq_pad, pair), F32),
                        pltpu.VMEM((seq_pad, pair), F32), pltpu.VMEM((seq_pad, 2 * GLA_DV), F32),
                        pltpu.VMEM((seq_pad, 2 * GLA_DV), F32), pltpu.VMEM((4, GLA_DV, pair), F32)],
        compiler_params=_cparams(("parallel", "parallel")),
        name="gla",
    )(u_a, u_a, u_a, u_a, u_a, w2f, b_f.reshape(1, -1), w2b, b_b.reshape(1, -1), norm_g.reshape(1, -1))


LOG2E = 1.4426950408889634
SWA_Q_SCALE = SWA_HEAD_DIM ** -0.5 * LOG2E


def _swa_slope2(h):
    return LOG2E * 2.0 ** (-8.0 * (h + 1) / SWA_HEADS)


def _swa_kernel(sink_ref, q_ref, k_ref, v_ref, ng_ref, o_ref, band_bias, meta_bias):
    seq = q_ref.shape[1]
    blk = WINDOW
    n_blocks = (seq - N_META) // blk
    hd = SWA_HEAD_DIM
    grp = SWA_GROUP

    def stacked(fn, nq):
        row = lax.broadcasted_iota(jnp.int32, (grp * nq, 1), 0)
        col = jnp.full((grp * nq, 1), fn(grp - 1), F32)
        for g in range(grp - 2, -1, -1):
            col = jnp.where(row < (g + 1) * nq, fn(g), col)
        return col

    def band_bias_tile(kv, nq, nk, delta):
        qi = lax.broadcasted_iota(jnp.int32, (grp * nq, nk), 0) & (nq - 1)
        ki = lax.broadcasted_iota(jnp.int32, (grp * nq, nk), 1)
        dist = jnp.abs(qi - ki + delta)
        slope = stacked(lambda g: _swa_slope2(kv * grp + g), nq)
        return jnp.where(dist <= WINDOW, -slope * dist.astype(F32), NEG)

    def meta_bias_tile(kv, nq, among_meta):
        qi = lax.broadcasted_iota(jnp.int32, (grp * nq, blk), 0) & (nq - 1)
        ki = lax.broadcasted_iota(jnp.int32, (grp * nq, blk), 1)
        rel = jnp.abs(qi - ki) if among_meta else qi - ki
        slope = stacked(lambda g: _swa_slope2(kv * grp + g), nq)
        return jnp.where(ki < N_META, -slope * rel.astype(F32), NEG)

    def attend(r0, nq, w0, nk, band_fn, meta_fn):
        outs = [None] * SWA_HEADS
        for kv in range(SWA_KV_HEADS):
            ks = slice(kv * hd, (kv + 1) * hd)
            q4 = jnp.concatenate([q_ref[0, pl.ds(r0, nq), (kv * grp + g) * hd:(kv * grp + g + 1) * hd]
                                  for g in range(grp)], axis=0)
            s_band = _dot_nt(q4, k_ref[0, pl.ds(w0, nk), ks]) + band_fn(kv)
            s_meta = _dot_nt(q4, k_ref[0, 0:blk, ks]) + meta_fn(kv)
            sink = stacked(lambda g: sink_ref[kv * grp + g] * LOG2E, nq)
            m = jnp.maximum(jnp.maximum(jnp.max(s_band, axis=-1, keepdims=True),
                                        jnp.max(s_meta, axis=-1, keepdims=True)), sink)
            p_band = jnp.exp2(s_band - m)
            p_meta = jnp.exp2(s_meta - m)
            den = (jnp.sum(p_band, axis=-1, keepdims=True) + jnp.sum(p_meta, axis=-1, keepdims=True)
                   + jnp.exp2(sink - m))
            o4 = (_dot(p_band.astype(BF16), v_ref[0, pl.ds(w0, nk), ks])
                  + _dot(p_meta.astype(BF16), v_ref[0, 0:blk, ks])) * (1.0 / den)
            for g in range(grp):
                outs[kv * grp + g] = o4[g * nq:(g + 1) * nq]
        ssq = outs[0] * outs[0]
        for o in outs[1:]:
            ssq = ssq + o * o
        inv = lax.rsqrt(jnp.sum(ssq, axis=-1, keepdims=True) * (1.0 / SWA_WIDTH) + RMS_EPS)
        for h in range(SWA_HEADS):
            cols = slice(h * hd, (h + 1) * hd)
            o_ref[0, pl.ds(r0, nq), cols] = (outs[h] * inv * ng_ref[:, cols]).astype(o_ref.dtype)

    for kv in range(SWA_KV_HEADS):
        band_bias[kv] = band_bias_tile(kv, blk, 3 * blk, blk)
        meta_bias[kv] = meta_bias_tile(kv, blk, False)

    def meta_fn_at(r0):
        def fn(kv):
            slope = stacked(lambda g: _swa_slope2(kv * grp + g), blk)
            return meta_bias[kv] - slope * r0
        return fn

    last_r0 = N_META + (n_blocks - 1) * blk
    attend(0, N_META, N_META, blk, lambda kv: band_bias_tile(kv, N_META, blk, -N_META),
           lambda kv: meta_bias_tile(kv, N_META, True))
    attend(N_META, blk, N_META, 3 * blk, lambda kv: band_bias_tile(kv, blk, 3 * blk, 0), meta_fn_at(float(N_META)))
    attend(last_r0, blk, last_r0 - 2 * blk, 3 * blk, lambda kv: band_bias_tile(kv, blk, 3 * blk, 2 * blk),
           meta_fn_at(float(last_r0)))

    def body(j, carry):
        r0 = pl.multiple_of(N_META + j * blk, N_META)
        attend(r0, blk, pl.multiple_of(r0 - blk, N_META), 3 * blk, lambda kv: band_bias[kv],
               meta_fn_at(r0.astype(F32)))
        return carry

    lax.fori_loop(1, n_blocks - 1, body, 0)


def swa_branch(u_b, sink, norm_g):
    bsz, seq, _ = u_b.shape
    kvw = SWA_KV_HEADS * SWA_HEAD_DIM
    rows = SWA_GROUP * WINDOW
    return pl.pallas_call(
        _swa_kernel,
        grid=(bsz,),
        in_specs=[pl.BlockSpec(memory_space=pltpu.SMEM),
                  pl.BlockSpec((1, seq, SWA_WIDTH), lambda b: (b, 0, 0)),
                  pl.BlockSpec((1, seq, kvw), lambda b: (b, 0, SWA_WIDTH // kvw)),
                  pl.BlockSpec((1, seq, kvw), lambda b: (b, 0, SWA_WIDTH // kvw + 1)),
                  pl.BlockSpec((1, SWA_WIDTH), lambda b: (0, 0))],
        out_specs=pl.BlockSpec((1, seq, SWA_WIDTH), lambda b: (b, 0, 0)),
        out_shape=jax.ShapeDtypeStruct((bsz, seq, SWA_WIDTH), BF16),
        scratch_shapes=[pltpu.VMEM((SWA_KV_HEADS, rows, 3 * WINDOW), F32),
                        pltpu.VMEM((SWA_KV_HEADS, rows, WINDOW), F32)],
        compiler_params=_cparams(("parallel",)),
        name="swa",
    )(sink, u_b, u_b, u_b, norm_g.reshape(1, -1))


@functools.lru_cache(maxsize=None)
def _hyena_tables(seq):
    t = np.linspace(0.0, 1.0, seq, dtype=np.float32)[:, None]
    w = (np.float32(2.0 * math.pi) * np.arange(seq, dtype=np.float32)[:, None] / np.float32(seq)).astype(np.float32)
    bands = np.linspace(1e-4, HY_BANDS - 1, HY_BANDS, dtype=np.float32)
    arg = (bands * w).astype(np.float32).astype(np.float64)
    z = np.concatenate([t.astype(np.float64), np.cos(arg), -np.sin(arg)], axis=-1)
    z_pad = np.zeros((seq, LANES), np.float32)
    z_pad[:, :HY_EMB] = z.astype(np.float32)
    max_decay = math.log(HY_TARGET) / HY_FAST_PCT
    min_decay = math.log(HY_TARGET) / HY_SLOW_PCT
    rates = np.abs(np.linspace(min_decay, max_decay, HY_WIDTH, dtype=np.float32))[None, :]
    n = 2 * seq - 1
    fj = (np.arange(seq, dtype=np.int64)[:, None] * np.arange(seq, dtype=np.int64)[None, :]) % n
    ang = fj.astype(np.float64) * (2.0 * math.pi / n)
    return z_pad, rates.astype(np.float32), np.cos(ang).astype(np.float32), np.sin(ang).astype(np.float32)


def _hy_filter_kernel(z_ref, rate_ref, w1_ref, b1_ref, fr_ref, w2_ref, b2_ref, w3_ref, hs_ref, hd_ref):
    z = z_ref[...]
    h = jnp.sin(fr_ref[0, 0:1, :] * (_dot(z, w1_ref[0], precision=HIGHEST) + b1_ref[0]))
    h = jnp.sin(fr_ref[0, 1:2, :] * (_dot(h, w2_ref[0], precision=HIGHEST) + b2_ref[0]))
    h = _dot(h, w3_ref[0], precision=HIGHEST)
    window = jnp.exp(-z[:, 0:1] * rate_ref[...])
    h_fwd = h[:, :HY_WIDTH] * window
    h_bwd = h[:, HY_WIDTH:] * window
    first = lax.broadcasted_iota(jnp.int32, h_bwd.shape, 0) == 0
    h_bwd = jnp.where(first, 0.0, h_bwd)
    hs_ref[0] = h_fwd + h_bwd
    hd_ref[0] = h_bwd - h_fwd


def hyena_filters(z, rates, w1p, b1, freq, w2, b2, w3):
    depth = w1p.shape[0]
    seq = z.shape[0]

    def per_layer(*shape):
        return pl.BlockSpec((1,) + shape, lambda l: (l,) + (0,) * len(shape))

    out = jax.ShapeDtypeStruct((depth, seq, HY_WIDTH), F32)
    return pl.pallas_call(
        _hy_filter_kernel,
        grid=(depth,),
        in_specs=[pl.BlockSpec((seq, LANES), lambda l: (0, 0)), pl.BlockSpec((1, HY_WIDTH), lambda l: (0, 0)),
                  per_layer(LANES, HY_FFN), per_layer(1, HY_FFN), per_layer(2, HY_FFN),
                  per_layer(HY_FFN, HY_FFN), per_layer(1, HY_FFN), per_layer(HY_FFN, 2 * HY_WIDTH)],
        out_specs=[per_layer(seq, HY_WIDTH), per_layer(seq, HY_WIDTH)],
        out_shape=[out, out],
        compiler_params=_cparams(("parallel",)),
        name="hyena_filters",
    )(z, rates, w1p, b1, freq, w2, b2, w3)


def _split_bf16(x):
    hi = x.astype(BF16)
    return hi, (x - hi.astype(F32)).astype(BF16)


def _hy_spectrum_kernel(cos_ref, sin_ref, hs_ref, hd_ref, kc_ref, ks_ref):
    seq = cos_ref.shape[0]
    n = 2 * seq - 1
    f = lax.broadcasted_iota(jnp.int32, (seq, 1), 0)
    weight = jnp.where(f == 0, 1.0 / n, 2.0 / n)
    s_hi, s_lo = _split_bf16(hs_ref[0])
    d_hi, d_lo = _split_bf16(hd_ref[0])
    kc_ref[0] = (_dot(cos_ref[...], s_hi) + _dot(cos_ref[...], s_lo)) * weight
    ks_ref[0] = (_dot(sin_ref[...], d_hi) + _dot(sin_ref[...], d_lo)) * weight


def hyena_spectrum(cos_t, sin_t, h_sum, h_diff):
    depth, seq, _ = h_sum.shape
    half = HY_WIDTH // 2
    table = _resident((seq, seq), lambda l, c: (0, 0))
    blk = pl.BlockSpec((1, seq, half), lambda l, c: (l, 0, c))
    out = jax.ShapeDtypeStruct((depth, seq, HY_WIDTH), F32)
    return pl.pallas_call(
        _hy_spectrum_kernel,
        grid=(depth, 2),
        in_specs=[table, table, blk, blk],
        out_specs=[blk, blk],
        out_shape=[out, out],
        compiler_params=_cparams(("parallel", "parallel")),
        name="hyena_spectrum",
    )(cos_t, sin_t, h_sum, h_diff)


def _short_conv(u, w, b):
    seq = u.shape[0]
    row = lax.broadcasted_iota(jnp.int32, u.shape, 0)
    prev = jnp.where(row == 0, 0.0, pltpu.roll(u, 1, axis=0))
    nxt = jnp.where(row == seq - 1, 0.0, pltpu.roll(u, seq - 1, axis=0))
    return prev * w[0:1, :] + u * w[1:2, :] + nxt * w[2:3, :] + b


def _hyena_kernel(x0_ref, x1_ref, v_ref, w0_ref, w1_ref, wv_ref, b0_ref, b1_ref, bv_ref,
                  cos_ref, sin_ref, kc_ref, ks_ref, skip_ref, ng_ref, o_ref, acc):
    c = pl.program_id(1)
    half = x0_ref.shape[2]
    x0 = _short_conv(x0_ref[0], w0_ref[...], b0_ref[...])
    x1 = _short_conv(x1_ref[0], w1_ref[...], b1_ref[...])
    v = _short_conv(v_ref[0], wv_ref[...], bv_ref[...])
    g = x1 * v
    gb = g.astype(BF16)
    uc = _dot(cos_ref[...], gb)
    us = _dot(sin_ref[...], gb)
    kc = kc_ref[0]
    ks = ks_ref[0]
    pr = (uc * kc + us * ks).astype(BF16)
    pi = (uc * ks - us * kc).astype(BF16)
    y = _dot(cos_ref[...], pr) - _dot(sin_ref[...], pi) + g * skip_ref[...]
    y = x0 * y

    @pl.when(c == 0)
    def _():
        acc[:, 0:half] = y

    @pl.when(c == 1)
    def _():
        acc[:, half:2 * half] = y
        full = acc[...]
        o_ref[0] = (full * lax.rsqrt(jnp.mean(full * full, axis=-1, keepdims=True) + RMS_EPS)
                    * ng_ref[...]).astype(o_ref.dtype)


def hyena_branch(u_c, conv_w, conv_b, cos_t, sin_t, k_cos, k_sin, layer, skip, norm_g):
    bsz, seq, _ = u_c.shape
    half = HY_WIDTH // 2

    def stream(first_block):
        return pl.BlockSpec((1, seq, half), lambda b, c: (b, 0, first_block + c))

    def wcol(rows, first_block):
        return pl.BlockSpec((rows, half), lambda b, c: (0, first_block + c))

    table = _resident((seq, seq), lambda b, c: (0, 0))
    spec = pl.BlockSpec((1, seq, half), lambda b, c: (layer, 0, c))
    return pl.pallas_call(
        _hyena_kernel,
        grid=(bsz, 2),
        in_specs=[stream(0), stream(2), stream(4), wcol(3, 0), wcol(3, 2), wcol(3, 4),
                  wcol(1, 0), wcol(1, 2), wcol(1, 4), table, table, spec, spec, wcol(1, 0),
                  pl.BlockSpec((1, HY_WIDTH), lambda b, c: (0, 0))],
        out_specs=pl.BlockSpec((1, seq, HY_WIDTH), lambda b, c: (b, 0, 0)),
        out_shape=jax.ShapeDtypeStruct((bsz, seq, HY_WIDTH), BF16),
        scratch_shapes=[pltpu.VMEM((seq, HY_WIDTH), F32)],
        compiler_params=_cparams(("parallel", "arbitrary")),
        name="hyena",
    )(u_c, u_c, u_c, conv_w, conv_w, conv_w, conv_b, conv_b, conv_b, cos_t, sin_t, k_cos, k_sin,
      skip.reshape(1, -1), norm_g.reshape(1, -1))


def _first_index_of_max(x, valid, lane):
    m = jnp.max(jnp.where(valid, x, NEG), axis=-1, keepdims=True)
    idx = jnp.min(jnp.where(valid & (x == m), lane, float(LANES)), axis=-1, keepdims=True)
    return m, idx


def _route(x):
    lane = lax.broadcasted_iota(jnp.int32, x.shape, 1).astype(F32)
    is_group = lane < N_GROUPS
    gm, g_idx = _first_index_of_max(x, is_group, lane)
    g_top = 1.0 / jnp.sum(jnp.where(is_group, jnp.exp(x - gm), 0.0), axis=-1, keepdims=True)
    lo = N_GROUPS + g_idx * EXPERTS_PER_GROUP
    in_group = (lane >= lo) & (lane < lo + EXPERTS_PER_GROUP)
    m1, i1 = _first_index_of_max(x, in_group, lane)
    m2, i2 = _first_index_of_max(x, in_group & (lane != i1), lane)
    e2 = jnp.exp(m2 - m1)
    w1 = g_top / (1.0 + e2)
    w2 = g_top * e2 / (1.0 + e2)
    return jnp.where(lane == 0, i1 - N_GROUPS,
                     jnp.where(lane == 1, i2 - N_GROUPS,
                               jnp.where(lane == 2, w1, jnp.where(lane == 3, w2, 0.0))))


def _outproj_kernel(alpha, ya_ref, yb_ref, yc_ref, h_ref, w_ref, b_ref, g_ref, be_ref, wr_ref, br_ref,
                    h1_ref, rt_ref):
    na = ya_ref.shape[1]
    nb = yb_ref.shape[1]
    mix = (_dot(ya_ref[...], w_ref[0, 0:na, :]) + _dot(yb_ref[...], w_ref[0, na:na + nb, :])
           + _dot(yc_ref[...], w_ref[0, na + nb:, :]) + b_ref[...])
    h1 = _ln(alpha * h_ref[...] + mix, g_ref[...], be_ref[...])
    h1_ref[...] = h1
    rt_ref[...] = _route(_dot(h1.astype(BF16), wr_ref[0]) + br_ref[0])


def out_projection(alpha, y_a, y_b, y_c, h, w, layer, b, g, be, w_router, b_router):
    t, d = h.shape

    def row(width):
        return pl.BlockSpec((ROW_TILE, width), lambda i: (i, 0))

    def vec():
        return pl.BlockSpec((1, d), lambda i: (0, 0))

    return pl.pallas_call(
        functools.partial(_outproj_kernel, alpha),
        grid=(t // ROW_TILE,),
        in_specs=[row(y_a.shape[1]), row(y_b.shape[1]), row(y_c.shape[1]), row(d),
                  _resident((1, d, d), lambda i: (layer, 0, 0)), vec(), vec(), vec(),
                  _resident((1, d, LANES), lambda i: (layer, 0, 0)),
                  pl.BlockSpec((1, 1, LANES), lambda i: (layer, 0, 0))],
        out_specs=[row(d), row(LANES)],
        out_shape=[jax.ShapeDtypeStruct((t, d), F32), jax.ShapeDtypeStruct((t, LANES), F32)],
        compiler_params=_cparams(("parallel",)),
        name="out_proj_ln",
    )(y_a, y_b, y_c, h, w, b.reshape(1, d), g.reshape(1, d), be.reshape(1, d), w_router, b_router)


def dispatch_plan(expert_ids, n_steps):
    t = expert_ids.shape[0]
    flat = expert_ids.reshape(-1)
    onehot = (flat[:, None] == jnp.arange(N_EXPERTS, dtype=jnp.int32)[None, :]).astype(jnp.int32)
    counts = jnp.sum(onehot, axis=0)
    rank = jnp.sum((jnp.cumsum(onehot, axis=0) - onehot) * onehot, axis=1)
    padded = (counts + MOE_TILE - 1) // MOE_TILE * MOE_TILE
    ends = jnp.cumsum(padded)
    starts = ends - padded
    pos = starts[flat] + rank
    spare = 2 * t + jnp.arange((n_steps + 1) * MOE_TILE, dtype=jnp.int32) % MOE_TILE
    row_assign = spare.at[pos + MOE_TILE].set(jnp.arange(2 * t, dtype=jnp.int32))
    plane = t + MOE_TILE // 2
    row_token = jnp.minimum(row_assign >> 1, t - 1)
    row_dest = (row_assign >> 1) + (row_assign & 1) * plane
    n_used = (ends[-1] // MOE_TILE).astype(jnp.int32)
    tile_start = jnp.minimum(jnp.arange(n_steps, dtype=jnp.int32), n_used - 1) * MOE_TILE
    tile_expert = jnp.sum((tile_start[:, None] >= ends[None, :]).astype(jnp.int32), axis=1)
    return row_token, row_dest, tile_expert.astype(jnp.int32), n_used.reshape(1)


def _moe_kernel(te_ref, nu_ref, rt_ref, rd_ref, x_hbm, wg_ref, wu_ref, wd_ref, y_hbm, xbuf, ybuf, gsem, ssem):
    i = pl.program_id(0)
    n_used = nu_ref[0]
    tm = xbuf.shape[1]

    def gather_copy(tile, slot, r):
        tok = rt_ref[(tile + 1) * tm + r]
        return pltpu.make_async_copy(x_hbm.at[pl.ds(tok, 1)], xbuf.at[slot, pl.ds(r, 1)], gsem.at[slot])

    def scatter_copy(tile, slot, r):
        dst = rd_ref[(tile + 1) * tm + r]
        return pltpu.make_async_copy(ybuf.at[slot, pl.ds(r, 1)], y_hbm.at[pl.ds(dst, 1)], ssem.at[slot])

    def wait_gather(slot):
        pltpu.make_async_copy(x_hbm.at[pl.ds(0, tm)], xbuf.at[slot], gsem.at[slot]).wait()

    def wait_scatter(slot):
        pltpu.make_async_copy(ybuf.at[slot], y_hbm.at[pl.ds(0, tm)], ssem.at[slot]).wait()

    def looped(copy_fn, tile, slot):
        def body(r, carry):
            copy_fn(tile, slot, r).start()
            return carry
        lax.fori_loop(0, tm, body, 0)

    @pl.when(i == 0)
    def _():
        ybuf[...] = jnp.zeros(ybuf.shape, ybuf.dtype)
        looped(gather_copy, 0, 0)

    for cur in range(2):
        nxt = 1 - cur

        @pl.when((i % 2 == cur) & (i <= n_used))
        def _():
            wait_gather(cur)

        @pl.when((i % 2 == cur) & (i >= 1) & (i <= n_used))
        def _():
            wait_scatter(cur)

        @pl.when((i % 2 == cur) & (i < n_used))
        def _():
            x = xbuf[cur].astype(BF16)

            def neighbour_copies(lo, hi):
                for r in range(lo, hi):
                    gather_copy(i + 1, nxt, r).start()
                    scatter_copy(i - 1, nxt, r).start()

            quarter = tm // 4
            neighbour_copies(0, quarter)
            a = _dot(x, wg_ref[0])
            neighbour_copies(quarter, 2 * quarter)
            u = _dot(x, wu_ref[0])
            neighbour_copies(2 * quarter, 3 * quarter)
            hmid = (a * (1.0 / (1.0 + jnp.exp(-a))) * u).astype(BF16)
            neighbour_copies(3 * quarter, tm)
            ybuf[cur] = _dot(hmid, wd_ref[0])

        @pl.when((i % 2 == cur) & (i == n_used))
        def _():
            looped(scatter_copy, i - 1, nxt)
            wait_scatter(nxt)


def expert_mlps(x, w_gate, w_up, w_down, tile_expert, n_used, row_token, row_dest):
    t, d = x.shape
    n_steps = tile_expert.shape[0]
    f = w_gate.shape[2]
    grid_spec = pltpu.PrefetchScalarGridSpec(
        num_scalar_prefetch=4,
        grid=(n_steps,),
        in_specs=[pl.BlockSpec(memory_space=pl.ANY),
                  pl.BlockSpec((1, d, f), lambda i, te, nu, rt, rd: (te[i], 0, 0)),
                  pl.BlockSpec((1, d, f), lambda i, te, nu, rt, rd: (te[i], 0, 0)),
                  pl.BlockSpec((1, f, d), lambda i, te, nu, rt, rd: (te[i], 0, 0))],
        out_specs=pl.BlockSpec(memory_space=pl.ANY),
        scratch_shapes=[pltpu.VMEM((2, MOE_TILE, d), F32), pltpu.VMEM((2, MOE_TILE, d), F32),
                        pltpu.SemaphoreType.DMA((2,)), pltpu.SemaphoreType.DMA((2,))],
    )
    return pl.pallas_call(
        _moe_kernel,
        grid_spec=grid_spec,
        out_shape=jax.ShapeDtypeStruct((2 * (t + MOE_TILE // 2), d), F32),
        compiler_params=_cparams(("arbitrary",)),
        name="moe_experts",
    )(tile_expert, n_used, row_token, row_dest, x, w_gate, w_up, w_down).reshape(2, t + MOE_TILE // 2, d)


def _combine_kernel(alpha, y1_ref, y2_ref, h_ref, rw_ref, g_ref, b_ref, o_ref, ob_ref):
    rw = rw_ref[...]
    moe = y1_ref[0] * rw[:, 2:3] + y2_ref[0] * rw[:, 3:4]
    y = _ln(alpha * h_ref[...] + moe, g_ref[...], b_ref[...])
    o_ref[...] = y
    ob_ref[...] = y.astype(BF16)


def combine_ln(alpha, y_assign, h, route_out, g, b):
    t, d = h.shape
    tm = ROW_TILE
    row = pl.BlockSpec((tm, d), lambda i: (i, 0))
    vec = pl.BlockSpec((1, d), lambda i: (0, 0))
    return pl.pallas_call(
        functools.partial(_combine_kernel, alpha),
        grid=(t // tm,),
        in_specs=[pl.BlockSpec((1, tm, d), lambda i: (0, i, 0)), pl.BlockSpec((1, tm, d), lambda i: (1, i, 0)), row,
                  pl.BlockSpec((tm, LANES), lambda i: (i, 0)), vec, vec],
        out_specs=[row, row],
        out_shape=[jax.ShapeDtypeStruct((t, d), F32), jax.ShapeDtypeStruct((t, d), BF16)],
        compiler_params=_cparams(("parallel",)),
        name="moe_combine_ln",
    )(y_assign, y_assign, h, route_out, g.reshape(1, d), b.reshape(1, d))


def kernel(x, meta, emb_ln_g, emb_ln_b, w_in, b_in, gla_w2_f, gla_b_f, gla_w2_b, gla_b_b, gla_norm_g,
           swa_sink, swa_norm_g, hy_conv_w, hy_conv_b, hy_w1, hy_b1, hy_freq, hy_w2, hy_b2, hy_w3, hy_skip,
           hy_norm_g, w_out, b_out, ln1_g, ln1_b, router_wg, router_bg, router_we, router_be,
           exp_w_gate, exp_w_up, exp_w_down, ln2_g, ln2_b):
    bsz, seq_in, d = x.shape
    depth = w_in.shape[0]
    seq = seq_in + N_META
    t = bsz * seq
    alpha = (2.0 * depth) ** 0.25

    o_gates, o_r, o_swa, o_hy = 1024, 1056, 1568, 3104
    pad_cols = A_COLS - (o_swa)
    w_a = jnp.concatenate([w_in[:, :, :o_gates], w_in[:, :, o_r:o_swa], w_in[:, :, o_gates:o_r],
                           jnp.zeros((depth, d, pad_cols), w_in.dtype)], axis=-1).astype(BF16)
    b_a = jnp.concatenate([b_in[:, :o_gates], b_in[:, o_r:o_swa], b_in[:, o_gates:o_r],
                           jnp.zeros((depth, pad_cols), b_in.dtype)], axis=-1)
    q_scale = jnp.concatenate([jnp.full((SWA_WIDTH,), SWA_Q_SCALE, F32), jnp.ones((o_hy - o_swa - SWA_WIDTH,), F32)])
    w_b = (w_in[:, :, o_swa:o_hy] * q_scale).astype(BF16)
    b_b = b_in[:, o_swa:o_hy] * q_scale
    w_c = w_in[:, :, o_hy:].astype(BF16)
    w_out_b = w_out.astype(BF16)
    w2f = jnp.zeros((depth, LANES, gla_w2_f.shape[2]), F32).at[:, :GLA_RANK].set(gla_w2_f)
    w2b = jnp.zeros((depth, LANES, gla_w2_b.shape[2]), F32).at[:, GLA_RANK:2 * GLA_RANK].set(gla_w2_b)
    n_r = N_GROUPS + N_EXPERTS
    w_router = jnp.zeros((depth, d, LANES), F32).at[:, :, :N_GROUPS].set(router_wg)
    w_router = w_router.at[:, :, N_GROUPS:n_r].set(router_we).astype(BF16)
    b_router = jnp.zeros((depth, 1, LANES), F32).at[:, 0, :N_GROUPS].set(router_bg)
    b_router = b_router.at[:, 0, N_GROUPS:n_r].set(router_be)
    d_exp = exp_w_gate.shape[-1]
    wg_b = exp_w_gate.astype(BF16).reshape(depth * N_EXPERTS, d, d_exp)
    wu_b = exp_w_up.astype(BF16).reshape(depth * N_EXPERTS, d, d_exp)
    wd_b = exp_w_down.astype(BF16).reshape(depth * N_EXPERTS, d_exp, d)

    z_np, rates_np, cos_np, sin_np = _hyena_tables(seq)
    cos_t = jnp.asarray(cos_np).astype(BF16)
    sin_t = jnp.asarray(sin_np).astype(BF16)
    w1p = jnp.zeros((depth, LANES, HY_FFN), F32).at[:, :HY_EMB].set(hy_w1)
    h_sum, h_diff = hyena_filters(jnp.asarray(z_np), jnp.asarray(rates_np), w1p, hy_b1.reshape(depth, 1, HY_FFN),
                                  hy_freq, hy_w2, hy_b2.reshape(depth, 1, HY_FFN), hy_w3)
    k_cos, k_sin = hyena_spectrum(cos_t, sin_t, h_sum, h_diff)

    n_steps = (2 * t + N_EXPERTS * (MOE_TILE - 1)) // MOE_TILE + 2

    tokens = jnp.concatenate([jnp.broadcast_to(meta.astype(x.dtype)[None], (bsz, N_META, d)), x], axis=1)
    h, hb = ln_rows(tokens.reshape(t, d), emb_ln_g, emb_ln_b)
    for l in range(depth):
        u_a = project(hb, w_a, l, b_a[l].reshape(1, -1), F32).reshape(bsz, seq, -1)
        u_b = project(hb, w_b, l, b_b[l].reshape(1, -1), BF16).reshape(bsz, seq, -1)
        u_c = project(hb, w_c, l, b_in[l, o_hy:].reshape(1, -1), F32).reshape(bsz, seq, -1)
        y_a = gla_branch(u_a, w2f[l], gla_b_f[l], w2b[l], gla_b_b[l], gla_norm_g[l])
        y_b = swa_branch(u_b, swa_sink[l], swa_norm_g[l])
        y_c = hyena_branch(u_c, hy_conv_w[l], hy_conv_b[l].reshape(1, -1), cos_t, sin_t, k_cos, k_sin, l,
                           hy_skip[l], hy_norm_g[l])
        h1, routed = out_projection(alpha, y_a.reshape(t, -1), y_b.reshape(t, -1), y_c.reshape(t, -1), h,
                                    w_out_b, l, b_out[l], ln1_g[l], ln1_b[l], w_router, b_router)
        expert_ids = routed[:, 0:2].astype(jnp.int32)
        row_token, row_dest, tile_expert, n_used = dispatch_plan(expert_ids, n_steps)
        y_assign = expert_mlps(h1, wg_b, wu_b, wd_b, tile_expert + l * N_EXPERTS, n_used, row_token, row_dest)
        h, hb = combine_ln(alpha, y_assign, h1, routed, ln2_g[l], ln2_b[l])
    return h.reshape(bsz, seq, d)[:, N_META:]
```

```python
import functools
import math

import jax
import jax.numpy as jnp
import numpy as np
from jax import lax
from jax.experimental import pallas as pl
from jax.experimental.pallas import tpu as pltpu

F32 = jnp.float32
BF16 = jnp.bfloat16
HIGHEST = lax.Precision.HIGHEST

D_MODEL = 2048
N_META = 16
GLA_HEADS = 4
GLA_DK = 64
GLA_DV = 128
GLA_WIDTH = GLA_HEADS * GLA_DV
GLA_RANK = 16
GLA_TAU = 16.0
GLA_CHUNK = 64
GLA_UNROLL = 11
SWA_HEADS = 8
SWA_KV_HEADS = 2
SWA_GROUP = SWA_HEADS // SWA_KV_HEADS
SWA_HEAD_DIM = 128
SWA_WIDTH = SWA_HEADS * SWA_HEAD_DIM
WINDOW = 128
HY_WIDTH = 512
HY_BANDS = 16
HY_EMB = 2 * HY_BANDS + 1
HY_FFN = 64
HY_FAST_PCT = 0.3
HY_SLOW_PCT = 1.5
HY_TARGET = 1e-2
N_GROUPS = 4
EXPERTS_PER_GROUP = 4
N_EXPERTS = N_GROUPS * EXPERTS_PER_GROUP
D_EXPERT = 1024
LN_EPS = 1e-5
RMS_EPS = 1e-6
NEG = -1e30

LANES = 128
A_COLS = 1664
GATE_COL0 = 1536
ROW_TILE = 384
MOE_TILE = 256
VMEM_LIMIT = 56 * 1024 * 1024


def _cparams(sem):
    return pltpu.CompilerParams(dimension_semantics=sem, vmem_limit_bytes=VMEM_LIMIT)


def _resident(shape, index_map):
    return pl.BlockSpec(shape, index_map, pipeline_mode=pl.Buffered(1))


def _dot(a, b, **kw):
    return jnp.dot(a, b, preferred_element_type=F32, **kw)


def _dot_nt(a, b):
    return lax.dot_general(a, b, (((1,), (1,)), ((), ())), preferred_element_type=F32)


def _ln(x, g, b):
    mu = jnp.mean(x, axis=-1, keepdims=True)
    xc = x - mu
    var = jnp.mean(xc * xc, axis=-1, keepdims=True)
    return xc * lax.rsqrt(var + LN_EPS) * g + b


def _ln_rows_kernel(x_ref, g_ref, b_ref, o_ref, ob_ref):
    y = _ln(x_ref[...], g_ref[...], b_ref[...])
    o_ref[...] = y
    ob_ref[...] = y.astype(BF16)


def ln_rows(x, g, b):
    t, d = x.shape
    row = pl.BlockSpec((ROW_TILE, d), lambda i: (i, 0))
    vec = pl.BlockSpec((1, d), lambda i: (0, 0))
    return pl.pallas_call(
        _ln_rows_kernel,
        grid=(t // ROW_TILE,),
        in_specs=[row, vec, vec],
        out_specs=[row, row],
        out_shape=[jax.ShapeDtypeStruct((t, d), F32), jax.ShapeDtypeStruct((t, d), BF16)],
        compiler_params=_cparams(("parallel",)),
        name="ln_rows",
    )(x, g.reshape(1, d), b.reshape(1, d))


def _proj_kernel(x_ref, w_ref, b_ref, o_ref):
    o_ref[...] = (_dot(x_ref[...], w_ref[0]) + b_ref[...]).astype(o_ref.dtype)


def project(x, w, layer, b, out_dtype):
    t, k = x.shape
    n = w.shape[2]
    return pl.pallas_call(
        _proj_kernel,
        grid=(t // ROW_TILE,),
        in_specs=[pl.BlockSpec((ROW_TILE, k), lambda i: (i, 0)),
                  _resident((1, k, n), lambda i: (layer, 0, 0)),
                  pl.BlockSpec((1, n), lambda i: (0, 0))],
        out_specs=pl.BlockSpec((ROW_TILE, n), lambda i: (i, 0)),
        out_shape=jax.ShapeDtypeStruct((t, n), out_dtype),
        compiler_params=_cparams(("parallel",)),
        name="in_proj",
    )(x, w, b)


def _log_sigmoid(x):
    return jnp.minimum(x, 0.0) - jnp.log(1.0 + jnp.exp(-jnp.abs(x)))


def _chunk_scan(x, reverse):
    c = x.shape[0]
    idx = lax.broadcasted_iota(jnp.int32, x.shape, 0)
    s = 1
    while s < c:
        if reverse:
            x = x + jnp.where(idx < c - s, pltpu.roll(x, c - s, axis=0), 0.0)
        else:
            x = x + jnp.where(idx >= s, pltpu.roll(x, s, axis=0), 0.0)
        s *= 2
    return x


def _gla_kernel(q_ref, k_ref, v_ref, r_ref, gt_ref, w2f_ref, bf_ref, w2b_ref, bb_ref, ng_ref, o_ref,
                qs, ks, vs, lfs, lbs, acc, acc_b, st):
    seq = q_ref.shape[1]
    seq_pad = qs.shape[0]
    n_chunks = seq_pad // GLA_CHUNK
    tail = seq_pad - seq
    c = GLA_CHUNK

    gates = gt_ref[0]
    log_f = _log_sigmoid(_dot(gates, w2f_ref[...], precision=HIGHEST) + bf_ref[...]) * (1.0 / GLA_TAU)
    log_b = _log_sigmoid(_dot(gates, w2b_ref[...], precision=HIGHEST) + bb_ref[...]) * (1.0 / GLA_TAU)
    for dst, src in ((qs, q_ref[0]), (ks, k_ref[0]), (vs, v_ref[0]), (lfs, log_f), (lbs, log_b)):
        dst[0:seq, :] = src
        dst[seq:seq_pad, :] = jnp.zeros((tail, dst.shape[1]), F32)

    lane = lax.broadcasted_iota(jnp.int32, (c, 2 * GLA_DK), 1)
    row = lax.broadcasted_iota(jnp.int32, (c, c), 0)
    col = lax.broadcasted_iota(jnp.int32, (c, c), 1)
    scale = GLA_DK ** -0.5

    st[...] = jnp.zeros(st.shape, F32)

    def chunk(n, log_ref, reverse, out_ref):
        keep = (col > row) if reverse else (col <= row)
        r0 = pl.multiple_of(n * c, c)
        cum = _chunk_scan(log_ref[pl.ds(r0, c), :], reverse)
        tot = cum[0:1, :] if reverse else cum[c - 1:c, :]
        q = qs[pl.ds(r0, c), :]
        k = ks[pl.ds(r0, c), :]
        q_dec = q * jnp.exp(cum) * scale
        k_inv = (k * jnp.exp(-cum)).astype(BF16)
        k_dec = (k * jnp.exp(tot - cum)).astype(BF16)
        decay = jnp.exp(tot)
        for h in range(2):
            slot = 2 * int(reverse) + h
            head_lanes = (lane < GLA_DK) if h == 0 else (lane >= GLA_DK)
            qm = jnp.where(head_lanes, q_dec, 0.0).astype(BF16)
            att = jnp.where(keep, _dot_nt(qm, k_inv), 0.0)
            vh = vs[pl.ds(r0, c), h * GLA_DV:(h + 1) * GLA_DV]
            state = st[slot]
            out_ref[pl.ds(r0, c), h * GLA_DV:(h + 1) * GLA_DV] = (
                _dot(att.astype(BF16), vh.astype(BF16)) + _dot_nt(qm, state.astype(BF16)))
            st[slot] = state * decay + _dot(vh.T.astype(BF16), k_dec)

    def body(i, carry):
        chunk(i, lfs, False, acc)
        chunk(n_chunks - 1 - i, lbs, True, acc_b)
        return carry

    lax.fori_loop(0, n_chunks, body, 0, unroll=GLA_UNROLL)

    r = r_ref[0]
    gate = r * (1.0 / (1.0 + jnp.exp(-r)))
    for h in range(2):
        sl = slice(h * GLA_DV, (h + 1) * GLA_DV)
        o = acc[0:seq, sl] + acc_b[0:seq, sl]
        y = o * lax.rsqrt(jnp.mean(o * o, axis=-1, keepdims=True) + RMS_EPS) * ng_ref[:, sl]
        o_ref[0, :, sl] = (y * gate[:, sl]).astype(o_ref.dtype)


def gla_branch(u_a, w2f, b_f, w2b, b_b, norm_g):
    bsz, seq, _ = u_a.shape
    seq_pad = -(-seq // GLA_CHUNK) * GLA_CHUNK
    pair = 2 * GLA_DK

    def col(width, first_block):
        return pl.BlockSpec((1, seq, width), lambda b, p: (b, 0, first_block + p))

    vec128 = pl.BlockSpec((1, pair), lambda b, p: (0, p))
    return pl.pallas_call(
        _gla_kernel,
        grid=(bsz, GLA_HEADS // 2),
        in_specs=[col(pair, 0), col(pair, 2), col(2 * GLA_DV, 2), col(2 * GLA_DV, 4),
                  pl.BlockSpec((1, seq, LANES), lambda b, p: (b, 0, GATE_COL0 // LANES)),
                  pl.BlockSpec((LANES, pair), lambda b, p: (0, p)), vec128,
                  pl.BlockSpec((LANES, pair), lambda b, p: (0, p)), vec128,
                  pl.BlockSpec((1, 2 * GLA_DV), lambda b, p: (0, p))],
        out_specs=pl.BlockSpec((1, seq, 2 * GLA_DV), lambda b, p: (b, 0, p)),
        out_shape=jax.ShapeDtypeStruct((bsz, seq, GLA_WIDTH), BF16),
        scratch_shapes=[pltpu.VMEM((seq_pad, pair), F32), pltpu.VMEM((seq_pad, pair), F32),
                        pltpu.VMEM((seq_pad, 2 * GLA_DV), F32), pltpu.VMEM((seq_pad, pair), F32),
                        pltpu.VMEM((seq_pad, pair), F32), pltpu.VMEM((seq_pad, 2 * GLA_DV), F32),
                        pltpu.VMEM((seq_pad, 2 * GLA_DV), F32), pltpu.VMEM((4, GLA_DV, pair), F32)],
        compiler_params=_cparams(("parallel", "parallel")),
        name="gla",
    )(u_a, u_a, u_a, u_a, u_a, w2f, b_f.reshape(1, -1), w2b, b_b.reshape(1, -1), norm_g.reshape(1, -1))


LOG2E = 1.4426950408889634
SWA_Q_SCALE = SWA_HEAD_DIM ** -0.5 * LOG2E


def _swa_slope2(h):
    return LOG2E * 2.0 ** (-8.0 * (h + 1) / SWA_HEADS)


def _swa_kernel(sink_ref, q_ref, k_ref, v_ref, ng_ref, o_ref, band_bias, meta_bias):
    seq = q_ref.shape[1]
    blk = WINDOW
    n_blocks = (seq - N_META) // blk
    hd = SWA_HEAD_DIM
    grp = SWA_GROUP

    def stacked(fn, nq):
        row = lax.broadcasted_iota(jnp.int32, (grp * nq, 1), 0)
        col = jnp.full((grp * nq, 1), fn(grp - 1), F32)
        for g in range(grp - 2, -1, -1):
            col = jnp.where(row < (g + 1) * nq, fn(g), col)
        return col

    def band_bias_tile(kv, nq, nk, delta):
        qi = lax.broadcasted_iota(jnp.int32, (grp * nq, nk), 0) & (nq - 1)
        ki = lax.broadcasted_iota(jnp.int32, (grp * nq, nk), 1)
        dist = jnp.abs(qi - ki + delta)
        slope = stacked(lambda g: _swa_slope2(kv * grp + g), nq)
        return jnp.where(dist <= WINDOW, -slope * dist.astype(F32), NEG)

    def meta_bias_tile(kv, nq, among_meta):
        qi = lax.broadcasted_iota(jnp.int32, (grp * nq, blk), 0) & (nq - 1)
        ki = lax.broadcasted_iota(jnp.int32, (grp * nq, blk), 1)
        rel = jnp.abs(qi - ki) if among_meta else qi - ki
        slope = stacked(lambda g: _swa_slope2(kv * grp + g), nq)
        return jnp.where(ki < N_META, -slope * rel.astype(F32), NEG)

    def attend(r0, nq, w0, nk, band_fn, meta_fn):
        outs = [None] * SWA_HEADS
        for kv in range(SWA_KV_HEADS):
            ks = slice(kv * hd, (kv + 1) * hd)
            q4 = jnp.concatenate([q_ref[0, pl.ds(r0, nq), (kv * grp + g) * hd:(kv * grp + g + 1) * hd]
                                  for g in range(grp)], axis=0)
            s_band = _dot_nt(q4, k_ref[0, pl.ds(w0, nk), ks]) + band_fn(kv)
            s_meta = _dot_nt(q4, k_ref[0, 0:blk, ks]) + meta_fn(kv)
            sink = stacked(lambda g: sink_ref[kv * grp + g] * LOG2E, nq)
            m = jnp.maximum(jnp.maximum(jnp.max(s_band, axis=-1, keepdims=True),
                                        jnp.max(s_meta, axis=-1, keepdims=True)), sink)
            p_band = jnp.exp2(s_band - m)
            p_meta = jnp.exp2(s_meta - m)
            den = (jnp.sum(p_band, axis=-1, keepdims=True) + jnp.sum(p_meta, axis=-1, keepdims=True)
                   + jnp.exp2(sink - m))
            o4 = (_dot(p_band.astype(BF16), v_ref[0, pl.ds(w0, nk), ks])
                  + _dot(p_meta.astype(BF16), v_ref[0, 0:blk, ks])) * (1.0 / den)
            for g in range(grp):
                outs[kv * grp + g] = o4[g * nq:(g + 1) * nq]
        ssq = outs[0] * outs[0]
        for o in outs[1:]:
            ssq = ssq + o * o
        inv = lax.rsqrt(jnp.sum(ssq, axis=-1, keepdims=True) * (1.0 / SWA_WIDTH) + RMS_EPS)
        for h in range(SWA_HEADS):
            cols = slice(h * hd, (h + 1) * hd)
            o_ref[0, pl.ds(r0, nq), cols] = (outs[h] * inv * ng_ref[:, cols]).astype(o_ref.dtype)

    for kv in range(SWA_KV_HEADS):
        band_bias[kv] = band_bias_tile(kv, blk, 3 * blk, blk)
        meta_bias[kv] = meta_bias_tile(kv, blk, False)

    def meta_fn_at(r0):
        def fn(kv):
            slope = stacked(lambda g: _swa_slope2(kv * grp + g), blk)
            return meta_bias[kv] - slope * r0
        return fn

    last_r0 = N_META + (n_blocks - 1) * blk
    attend(0, N_META, N_META, blk, lambda kv: band_bias_tile(kv, N_META, blk, -N_META),
           lambda kv: meta_bias_tile(kv, N_META, True))
    attend(N_META, blk, N_META, 3 * blk, lambda kv: band_bias_tile(kv, blk, 3 * blk, 0), meta_fn_at(float(N_META)))
    attend(last_r0, blk, last_r0 - 2 * blk, 3 * blk, lambda kv: band_bias_tile(kv, blk, 3 * blk, 2 * blk),
           meta_fn_at(float(last_r0)))

    def body(j, carry):
        r0 = pl.multiple_of(N_META + j * blk, N_META)
        attend(r0, blk, pl.multiple_of(r0 - blk, N_META), 3 * blk, lambda kv: band_bias[kv],
               meta_fn_at(r0.astype(F32)))
        return carry

    lax.fori_loop(1, n_blocks - 1, body, 0)


def swa_branch(u_b, sink, norm_g):
    bsz, seq, _ = u_b.shape
    kvw = SWA_KV_HEADS * SWA_HEAD_DIM
    rows = SWA_GROUP * WINDOW
    return pl.pallas_call(
        _swa_kernel,
        grid=(bsz,),
        in_specs=[pl.BlockSpec(memory_space=pltpu.SMEM),
                  pl.BlockSpec((1, seq, SWA_WIDTH), lambda b: (b, 0, 0)),
                  pl.BlockSpec((1, seq, kvw), lambda b: (b, 0, SWA_WIDTH // kvw)),
                  pl.BlockSpec((1, seq, kvw), lambda b: (b, 0, SWA_WIDTH // kvw + 1)),
                  pl.BlockSpec((1, SWA_WIDTH), lambda b: (0, 0))],
        out_specs=pl.BlockSpec((1, seq, SWA_WIDTH), lambda b: (b, 0, 0)),
        out_shape=jax.ShapeDtypeStruct((bsz, seq, SWA_WIDTH), BF16),
        scratch_shapes=[pltpu.VMEM((SWA_KV_HEADS, rows, 3 * WINDOW), F32),
                        pltpu.VMEM((SWA_KV_HEADS, rows, WINDOW), F32)],
        compiler_params=_cparams(("parallel",)),
        name="swa",
    )(sink, u_b, u_b, u_b, norm_g.reshape(1, -1))


@functools.lru_cache(maxsize=None)
def _hyena_tables(seq):
    t = np.linspace(0.0, 1.0, seq, dtype=np.float32)[:, None]
    w = (np.float32(2.0 * math.pi) * np.arange(seq, dtype=np.float32)[:, None] / np.float32(seq)).astype(np.float32)
    bands = np.linspace(1e-4, HY_BANDS - 1, HY_BANDS, dtype=np.float32)
    arg = (bands * w).astype(np.float32).astype(np.float64)
    z = np.concatenate([t.astype(np.float64), np.cos(arg), -np.sin(arg)], axis=-1)
    z_pad = np.zeros((seq, LANES), np.float32)
    z_pad[:, :HY_EMB] = z.astype(np.float32)
    max_decay = math.log(HY_TARGET) / HY_FAST_PCT
    min_decay = math.log(HY_TARGET) / HY_SLOW_PCT
    rates = np.abs(np.linspace(min_decay, max_decay, HY_WIDTH, dtype=np.float32))[None, :]
    n = 2 * seq - 1
    fj = (np.arange(seq, dtype=np.int64)[:, None] * np.arange(seq, dtype=np.int64)[None, :]) % n
    ang = fj.astype(np.float64) * (2.0 * math.pi / n)
    return z_pad, rates.astype(np.float32), np.cos(ang).astype(np.float32), np.sin(ang).astype(np.float32)


def _hy_filter_kernel(z_ref, rate_ref, w1_ref, b1_ref, fr_ref, w2_ref, b2_ref, w3_ref, hs_ref, hd_ref):
    z = z_ref[...]
    h = jnp.sin(fr_ref[0, 0:1, :] * (_dot(z, w1_ref[0], precision=HIGHEST) + b1_ref[0]))
    h = jnp.sin(fr_ref[0, 1:2, :] * (_dot(h, w2_ref[0], precision=HIGHEST) + b2_ref[0]))
    h = _dot(h, w3_ref[0], precision=HIGHEST)
    window = jnp.exp(-z[:, 0:1] * rate_ref[...])
    h_fwd = h[:, :HY_WIDTH] * window
    h_bwd = h[:, HY_WIDTH:] * window
    first = lax.broadcasted_iota(jnp.int32, h_bwd.shape, 0) == 0
    h_bwd = jnp.where(first, 0.0, h_bwd)
    hs_ref[0] = h_fwd + h_bwd
    hd_ref[0] = h_bwd - h_fwd


def hyena_filters(z, rates, w1p, b1, freq, w2, b2, w3):
    depth = w1p.shape[0]
    seq = z.shape[0]

    def per_layer(*shape):
        return pl.BlockSpec((1,) + shape, lambda l: (l,) + (0,) * len(shape))

    out = jax.ShapeDtypeStruct((depth, seq, HY_WIDTH), F32)
    return pl.pallas_call(
        _hy_filter_kernel,
        grid=(depth,),
        in_specs=[pl.BlockSpec((seq, LANES), lambda l: (0, 0)), pl.BlockSpec((1, HY_WIDTH), lambda l: (0, 0)),
                  per_layer(LANES, HY_FFN), per_layer(1, HY_FFN), per_layer(2, HY_FFN),
                  per_layer(HY_FFN, HY_FFN), per_layer(1, HY_FFN), per_layer(HY_FFN, 2 * HY_WIDTH)],
        out_specs=[per_layer(seq, HY_WIDTH), per_layer(seq, HY_WIDTH)],
        out_shape=[out, out],
        compiler_params=_cparams(("parallel",)),
        name="hyena_filters",
    )(z, rates, w1p, b1, freq, w2, b2, w3)


def _split_bf16(x):
    hi = x.astype(BF16)
    return hi, (x - hi.astype(F32)).astype(BF16)


def _hy_spectrum_kernel(cos_ref, sin_ref, hs_ref, hd_ref, kc_ref, ks_ref):
    seq = cos_ref.shape[0]
    n = 2 * seq - 1
    f = lax.broadcasted_iota(jnp.int32, (seq, 1), 0)
    weight = jnp.where(f == 0, 1.0 / n, 2.0 / n)
    s_hi, s_lo = _split_bf16(hs_ref[0])
    d_hi, d_lo = _split_bf16(hd_ref[0])
    kc_ref[0] = (_dot(cos_ref[...], s_hi) + _dot(cos_ref[...], s_lo)) * weight
    ks_ref[0] = (_dot(sin_ref[...], d_hi) + _dot(sin_ref[...], d_lo)) * weight


def hyena_spectrum(cos_t, sin_t, h_sum, h_diff):
    depth, seq, _ = h_sum.shape
    half = HY_WIDTH // 2
    table = _resident((seq, seq), lambda l, c: (0, 0))
    blk = pl.BlockSpec((1, seq, half), lambda l, c: (l, 0, c))
    out = jax.ShapeDtypeStruct((depth, seq, HY_WIDTH), F32)
    return pl.pallas_call(
        _hy_spectrum_kernel,
        grid=(depth, 2),
        in_specs=[table, table, blk, blk],
        out_specs=[blk, blk],
        out_shape=[out, out],
        compiler_params=_cparams(("parallel", "parallel")),
        name="hyena_spectrum",
    )(cos_t, sin_t, h_sum, h_diff)


def _short_conv(u, w, b):
    seq = u.shape[0]
    row = lax.broadcasted_iota(jnp.int32, u.shape, 0)
    prev = jnp.where(row == 0, 0.0, pltpu.roll(u, 1, axis=0))
    nxt = jnp.where(row == seq - 1, 0.0, pltpu.roll(u, seq - 1, axis=0))
    return prev * w[0:1, :] + u * w[1:2, :] + nxt * w[2:3, :] + b


def _hyena_kernel(x0_ref, x1_ref, v_ref, w0_ref, w1_ref, wv_ref, b0_ref, b1_ref, bv_ref,
                  cos_ref, sin_ref, kc_ref, ks_ref, skip_ref, ng_ref, o_ref, acc):
    c = pl.program_id(1)
    half = x0_ref.shape[2]
    x0 = _short_conv(x0_ref[0], w0_ref[...], b0_ref[...])
    x1 = _short_conv(x1_ref[0], w1_ref[...], b1_ref[...])
    v = _short_conv(v_ref[0], wv_ref[...], bv_ref[...])
    g = x1 * v
    gb = g.astype(BF16)
    uc = _dot(cos_ref[...], gb)
    us = _dot(sin_ref[...], gb)
    kc = kc_ref[0]
    ks = ks_ref[0]
    pr = (uc * kc + us * ks).astype(BF16)
    pi = (uc * ks - us * kc).astype(BF16)
    y = _dot(cos_ref[...], pr) - _dot(sin_ref[...], pi) + g * skip_ref[...]
    y = x0 * y

    @pl.when(c == 0)
    def _():
        acc[:, 0:half] = y

    @pl.when(c == 1)
    def _():
        acc[:, half:2 * half] = y
        full = acc[...]
        o_ref[0] = (full * lax.rsqrt(jnp.mean(full * full, axis=-1, keepdims=True) + RMS_EPS)
                    * ng_ref[...]).astype(o_ref.dtype)


def hyena_branch(u_c, conv_w, conv_b, cos_t, sin_t, k_cos, k_sin, layer, skip, norm_g):
    bsz, seq, _ = u_c.shape
    half = HY_WIDTH // 2

    def stream(first_block):
        return pl.BlockSpec((1, seq, half), lambda b, c: (b, 0, first_block + c))

    def wcol(rows, first_block):
        return pl.BlockSpec((rows, half), lambda b, c: (0, first_block + c))

    table = _resident((seq, seq), lambda b, c: (0, 0))
    spec = pl.BlockSpec((1, seq, half), lambda b, c: (layer, 0, c))
    return pl.pallas_call(
        _hyena_kernel,
        grid=(bsz, 2),
        in_specs=[stream(0), stream(2), stream(4), wcol(3, 0), wcol(3, 2), wcol(3, 4),
                  wcol(1, 0), wcol(1, 2), wcol(1, 4), table, table, spec, spec, wcol(1, 0),
                  pl.BlockSpec((1, HY_WIDTH), lambda b, c: (0, 0))],
        out_specs=pl.BlockSpec((1, seq, HY_WIDTH), lambda b, c: (b, 0, 0)),
        out_shape=jax.ShapeDtypeStruct((bsz, seq, HY_WIDTH), BF16),
        scratch_shapes=[pltpu.VMEM((seq, HY_WIDTH), F32)],
        compiler_params=_cparams(("parallel", "arbitrary")),
        name="hyena",
    )(u_c, u_c, u_c, conv_w, conv_w, conv_w, conv_b, conv_b, conv_b, cos_t, sin_t, k_cos, k_sin,
      skip.reshape(1, -1), norm_g.reshape(1, -1))


def _first_index_of_max(x, valid, lane):
    m = jnp.max(jnp.where(valid, x, NEG), axis=-1, keepdims=True)
    idx = jnp.min(jnp.where(valid & (x == m), lane, float(LANES)), axis=-1, keepdims=True)
    return m, idx


def _route(x):
    lane = lax.broadcasted_iota(jnp.int32, x.shape, 1).astype(F32)
    is_group = lane < N_GROUPS
    gm, g_idx = _first_index_of_max(x, is_group, lane)
    g_top = 1.0 / jnp.sum(jnp.where(is_group, jnp.exp(x - gm), 0.0), axis=-1, keepdims=True)
    lo = N_GROUPS + g_idx * EXPERTS_PER_GROUP
    in_group = (lane >= lo) & (lane < lo + EXPERTS_PER_GROUP)
    m1, i1 = _first_index_of_max(x, in_group, lane)
    m2, i2 = _first_index_of_max(x, in_group & (lane != i1), lane)
    e2 = jnp.exp(m2 - m1)
    w1 = g_top / (1.0 + e2)
    w2 = g_top * e2 / (1.0 + e2)
    return jnp.where(lane == 0, i1 - N_GROUPS,
                     jnp.where(lane == 1, i2 - N_GROUPS,
                               jnp.where(lane == 2, w1, jnp.where(lane == 3, w2, 0.0))))


def _outproj_kernel(alpha, ya_ref, yb_ref, yc_ref, h_ref, w_ref, b_ref, g_ref, be_ref, wr_ref, br_ref,
                    h1_ref, rt_ref):
    na = ya_ref.shape[1]
    nb = yb_ref.shape[1]
    mix = (_dot(ya_ref[...], w_ref[0, 0:na, :]) + _dot(yb_ref[...], w_ref[0, na:na + nb, :])
           + _dot(yc_ref[...], w_ref[0, na + nb:, :]) + b_ref[...])
    h1 = _ln(alpha * h_ref[...] + mix, g_ref[...], be_ref[...])
    h1_ref[...] = h1
    rt_ref[...] = _route(_dot(h1.astype(BF16), wr_ref[0]) + br_ref[0])


def out_projection(alpha, y_a, y_b, y_c, h, w, layer, b, g, be, w_router, b_router):
    t, d = h.shape

    def row(width):
        return pl.BlockSpec((ROW_TILE, width), lambda i: (i, 0))

    def vec():
        return pl.BlockSpec((1, d), lambda i: (0, 0))

    return pl.pallas_call(
        functools.partial(_outproj_kernel, alpha),
        grid=(t // ROW_TILE,),
        in_specs=[row(y_a.shape[1]), row(y_b.shape[1]), row(y_c.shape[1]), row(d),
                  _resident((1, d, d), lambda i: (layer, 0, 0)), vec(), vec(), vec(),
                  _resident((1, d, LANES), lambda i: (layer, 0, 0)),
                  pl.BlockSpec((1, 1, LANES), lambda i: (layer, 0, 0))],
        out_specs=[row(d), row(LANES)],
        out_shape=[jax.ShapeDtypeStruct((t, d), F32), jax.ShapeDtypeStruct((t, LANES), F32)],
        compiler_params=_cparams(("parallel",)),
        name="out_proj_ln",
    )(y_a, y_b, y_c, h, w, b.reshape(1, d), g.reshape(1, d), be.reshape(1, d), w_router, b_router)


def dispatch_plan(expert_ids, n_steps):
    t = expert_ids.shape[0]
    tm = MOE_TILE
    flat = expert_ids.reshape(-1)
    experts = jnp.arange(N_EXPERTS, dtype=jnp.int32)
    counts = jnp.sum((flat[:, None] == experts[None, :]).astype(jnp.int32), axis=0)
    padded = (counts + tm - 1) // tm * tm
    ends = jnp.cumsum(padded)
    pad_needed = jnp.arange(tm, dtype=jnp.int32)[None, :] < (padded - counts)[:, None]
    pad_keys = jnp.where(pad_needed, 2 * experts[:, None] + 1, 2 * N_EXPERTS).reshape(-1)
    keys = jnp.concatenate([2 * flat, pad_keys])
    vals = jnp.concatenate([jnp.arange(2 * t, dtype=jnp.int32), jnp.full((N_EXPERTS * tm,), 2 * t, jnp.int32)])
    _, ordered = lax.sort((keys, vals), num_keys=1)
    ordered = jnp.pad(ordered, (tm, (n_steps + 1) * tm - tm - ordered.shape[0]), constant_values=2 * t)
    lane = jnp.arange(ordered.shape[0], dtype=jnp.int32) % tm
    row_assign = jnp.where(ordered >= 2 * t, 2 * t + lane, ordered)
    plane = t + MOE_TILE // 2
    row_token = jnp.minimum(row_assign >> 1, t - 1)
    row_dest = (row_assign >> 1) + (row_assign & 1) * plane
    n_used = (ends[-1] // MOE_TILE).astype(jnp.int32)
    step = jnp.arange(n_steps, dtype=jnp.int32)
    tile_start = jnp.minimum(step, n_used - 1) * MOE_TILE
    tile_expert = jnp.sum((tile_start[:, None] >= ends[None, :]).astype(jnp.int32), axis=1)
    run_start = (step < n_used) & ((step == 0) | (tile_expert != jnp.roll(tile_expert, 1)))
    after = ends[tile_expert] // MOE_TILE
    next_expert = jnp.where(after < n_used, tile_expert[jnp.minimum(after, n_steps - 1)], -1)
    return (row_token, row_dest, tile_expert.astype(jnp.int32), n_used.reshape(1), run_start.astype(jnp.int32),
            next_expert.astype(jnp.int32))


def _moe_kernel(te_ref, nu_ref, rt_ref, rd_ref, rs_ref, ne_ref, x_hbm, wg_hbm, wu_hbm, wd_hbm, y_hbm,
                xbuf, ybuf, land_g, land_u, land_d, wg_ref, wu_ref, wd_ref, gsem, ssem, wsem):
    i = pl.program_id(0)
    n_used = nu_ref[0]
    tm = xbuf.shape[1]

    def weight_copies(e):
        return (pltpu.make_async_copy(wg_hbm.at[e], land_g, wsem.at[0]),
                pltpu.make_async_copy(wu_hbm.at[e], land_u, wsem.at[1]),
                pltpu.make_async_copy(wd_hbm.at[e], land_d, wsem.at[2]))

    @pl.when(i == 0)
    def _():
        for cp in weight_copies(te_ref[0]):
            cp.start()

    @pl.when(rs_ref[i] == 1)
    def _():
        for cp in weight_copies(te_ref[i]):
            cp.wait()
        wg_ref[...] = land_g[...].astype(BF16)
        wu_ref[...] = land_u[...].astype(BF16)
        wd_ref[...] = land_d[...].astype(BF16)

        @pl.when(ne_ref[i] >= 0)
        def _():
            for cp in weight_copies(ne_ref[i]):
                cp.start()

    def gather_copy(tile, slot, r):
        tok = rt_ref[(tile + 1) * tm + r]
        return pltpu.make_async_copy(x_hbm.at[pl.ds(tok, 1)], xbuf.at[slot, pl.ds(r, 1)], gsem.at[slot])

    def scatter_copy(tile, slot, r):
        dst = rd_ref[(tile + 1) * tm + r]
        return pltpu.make_async_copy(ybuf.at[slot, pl.ds(r, 1)], y_hbm.at[pl.ds(dst, 1)], ssem.at[slot])

    def wait_gather(slot):
        pltpu.make_async_copy(x_hbm.at[pl.ds(0, tm)], xbuf.at[slot], gsem.at[slot]).wait()

    def wait_scatter(slot):
        pltpu.make_async_copy(ybuf.at[slot], y_hbm.at[pl.ds(0, tm)], ssem.at[slot]).wait()

    def looped(copy_fn, tile, slot):
        def body(r, carry):
            copy_fn(tile, slot, r).start()
            return carry
        lax.fori_loop(0, tm, body, 0)

    @pl.when(i == 0)
    def _():
        ybuf[...] = jnp.zeros(ybuf.shape, ybuf.dtype)
        looped(gather_copy, 0, 0)

    for cur in range(2):
        nxt = 1 - cur

        @pl.when((i % 2 == cur) & (i <= n_used))
        def _():
            wait_gather(cur)

        @pl.when((i % 2 == cur) & (i >= 1) & (i <= n_used))
        def _():
            wait_scatter(cur)

        @pl.when((i % 2 == cur) & (i < n_used))
        def _():
            x = xbuf[cur].astype(BF16)

            def neighbour_copies(lo, hi):
                for r in range(lo, hi):
                    gather_copy(i + 1, nxt, r).start()
                    scatter_copy(i - 1, nxt, r).start()

            quarter = tm // 4
            neighbour_copies(0, quarter)
            a = _dot(x, wg_ref[...])
            neighbour_copies(quarter, 2 * quarter)
            u = _dot(x, wu_ref[...])
            neighbour_copies(2 * quarter, 3 * quarter)
            hmid = (a * (1.0 / (1.0 + jnp.exp(-a))) * u).astype(BF16)
            neighbour_copies(3 * quarter, tm)
            ybuf[cur] = _dot(hmid, wd_ref[...])

        @pl.when((i % 2 == cur) & (i == n_used))
        def _():
            looped(scatter_copy, i - 1, nxt)
            wait_scatter(nxt)


def expert_mlps(x, w_gate, w_up, w_down, tile_expert, n_used, row_token, row_dest, run_start, next_expert):
    t, d = x.shape
    n_steps = tile_expert.shape[0]
    f = w_gate.shape[2]
    hbm = pl.BlockSpec(memory_space=pl.ANY)
    grid_spec = pltpu.PrefetchScalarGridSpec(
        num_scalar_prefetch=6,
        grid=(n_steps,),
        in_specs=[hbm, hbm, hbm, hbm],
        out_specs=hbm,
        scratch_shapes=[pltpu.VMEM((2, MOE_TILE, d), F32), pltpu.VMEM((2, MOE_TILE, d), F32),
                        pltpu.VMEM((d, f), F32), pltpu.VMEM((d, f), F32), pltpu.VMEM((f, d), F32),
                        pltpu.VMEM((d, f), BF16), pltpu.VMEM((d, f), BF16), pltpu.VMEM((f, d), BF16),
                        pltpu.SemaphoreType.DMA((2,)), pltpu.SemaphoreType.DMA((2,)), pltpu.SemaphoreType.DMA((3,))],
    )
    return pl.pallas_call(
        _moe_kernel,
        grid_spec=grid_spec,
        out_shape=jax.ShapeDtypeStruct((2 * (t + MOE_TILE // 2), d), F32),
        compiler_params=_cparams(("arbitrary",)),
        name="moe_experts",
    )(tile_expert, n_used, row_token, row_dest, run_start, next_expert, x, w_gate, w_up, w_down
      ).reshape(2, t + MOE_TILE // 2, d)


def _combine_kernel(alpha, y1_ref, y2_ref, h_ref, rw_ref, g_ref, b_ref, o_ref, ob_ref):
    rw = rw_ref[...]
    moe = y1_ref[0] * rw[:, 2:3] + y2_ref[0] * rw[:, 3:4]
    y = _ln(alpha * h_ref[...] + moe, g_ref[...], b_ref[...])
    o_ref[...] = y
    ob_ref[...] = y.astype(BF16)


def combine_ln(alpha, y_assign, h, route_out, g, b):
    t, d = h.shape
    tm = ROW_TILE
    row = pl.BlockSpec((tm, d), lambda i: (i, 0))
    vec = pl.BlockSpec((1, d), lambda i: (0, 0))
    return pl.pallas_call(
        functools.partial(_combine_kernel, alpha),
        grid=(t // tm,),
        in_specs=[pl.BlockSpec((1, tm, d), lambda i: (0, i, 0)), pl.BlockSpec((1, tm, d), lambda i: (1, i, 0)), row,
                  pl.BlockSpec((tm, LANES), lambda i: (i, 0)), vec, vec],
        out_specs=[row, row],
        out_shape=[jax.ShapeDtypeStruct((t, d), F32), jax.ShapeDtypeStruct((t, d), BF16)],
        compiler_params=_cparams(("parallel",)),
        name="moe_combine_ln",
    )(y_assign, y_assign, h, route_out, g.reshape(1, d), b.reshape(1, d))


def kernel(x, meta, emb_ln_g, emb_ln_b, w_in, b_in, gla_w2_f, gla_b_f, gla_w2_b, gla_b_b, gla_norm_g,
           swa_sink, swa_norm_g, hy_conv_w, hy_conv_b, hy_w1, hy_b1, hy_freq, hy_w2, hy_b2, hy_w3, hy_skip,
           hy_norm_g, w_out, b_out, ln1_g, ln1_b, router_wg, router_bg, router_we, router_be,
           exp_w_gate, exp_w_up, exp_w_down, ln2_g, ln2_b):
    bsz, seq_in, d = x.shape
    depth = w_in.shape[0]
    seq = seq_in + N_META
    t = bsz * seq
    alpha = (2.0 * depth) ** 0.25

    o_gates, o_r, o_swa, o_hy = 1024, 1056, 1568, 3104
    pad_cols = A_COLS - (o_swa)
    w_a = jnp.concatenate([w_in[:, :, :o_gates], w_in[:, :, o_r:o_swa], w_in[:, :, o_gates:o_r],
                           jnp.zeros((depth, d, pad_cols), w_in.dtype)], axis=-1).astype(BF16)
    b_a = jnp.concatenate([b_in[:, :o_gates], b_in[:, o_r:o_swa], b_in[:, o_gates:o_r],
                           jnp.zeros((depth, pad_cols), b_in.dtype)], axis=-1)
    q_scale = jnp.concatenate([jnp.full((SWA_WIDTH,), SWA_Q_SCALE, F32), jnp.ones((o_hy - o_swa - SWA_WIDTH,), F32)])
    w_b = (w_in[:, :, o_swa:o_hy] * q_scale).astype(BF16)
    b_b = b_in[:, o_swa:o_hy] * q_scale
    w_c = w_in[:, :, o_hy:].astype(BF16)
    w_out_b = w_out.astype(BF16)
    w2f = jnp.zeros((depth, LANES, gla_w2_f.shape[2]), F32).at[:, :GLA_RANK].set(gla_w2_f)
    w2b = jnp.zeros((depth, LANES, gla_w2_b.shape[2]), F32).at[:, GLA_RANK:2 * GLA_RANK].set(gla_w2_b)
    n_r = N_GROUPS + N_EXPERTS
    w_router = jnp.zeros((depth, d, LANES), F32).at[:, :, :N_GROUPS].set(router_wg)
    w_router = w_router.at[:, :, N_GROUPS:n_r].set(router_we).astype(BF16)
    b_router = jnp.zeros((depth, 1, LANES), F32).at[:, 0, :N_GROUPS].set(router_bg)
    b_router = b_router.at[:, 0, N_GROUPS:n_r].set(router_be)
    d_exp = exp_w_gate.shape[-1]
    wg_all = exp_w_gate.reshape(depth * N_EXPERTS, d, d_exp)
    wu_all = exp_w_up.reshape(depth * N_EXPERTS, d, d_exp)
    wd_all = exp_w_down.reshape(depth * N_EXPERTS, d_exp, d)

    z_np, rates_np, cos_np, sin_np = _hyena_tables(seq)
    cos_t = jnp.asarray(cos_np).astype(BF16)
    sin_t = jnp.asarray(sin_np).astype(BF16)
    w1p = jnp.zeros((depth, LANES, HY_FFN), F32).at[:, :HY_EMB].set(hy_w1)
    h_sum, h_diff = hyena_filters(jnp.asarray(z_np), jnp.asarray(rates_np), w1p, hy_b1.reshape(depth, 1, HY_FFN),
                                  hy_freq, hy_w2, hy_b2.reshape(depth, 1, HY_FFN), hy_w3)
    k_cos, k_sin = hyena_spectrum(cos_t, sin_t, h_sum, h_diff)

    n_steps = (2 * t + N_EXPERTS * (MOE_TILE - 1)) // MOE_TILE + 2

    tokens = jnp.concatenate([jnp.broadcast_to(meta.astype(x.dtype)[None], (bsz, N_META, d)), x], axis=1)
    h, hb = ln_rows(tokens.reshape(t, d), emb_ln_g, emb_ln_b)
    for l in range(depth):
        u_a = project(hb, w_a, l, b_a[l].reshape(1, -1), F32).reshape(bsz, seq, -1)
        u_b = project(hb, w_b, l, b_b[l].reshape(1, -1), BF16).reshape(bsz, seq, -1)
        u_c = project(hb, w_c, l, b_in[l, o_hy:].reshape(1, -1), F32).reshape(bsz, seq, -1)
        y_a = gla_branch(u_a, w2f[l], gla_b_f[l], w2b[l], gla_b_b[l], gla_norm_g[l])
        y_b = swa_branch(u_b, swa_sink[l], swa_norm_g[l])
        y_c = hyena_branch(u_c, hy_conv_w[l], hy_conv_b[l].reshape(1, -1), cos_t, sin_t, k_cos, k_sin, l,
                           hy_skip[l], hy_norm_g[l])
        h1, routed = out_projection(alpha, y_a.reshape(t, -1), y_b.reshape(t, -1), y_c.reshape(t, -1), h,
                                    w_out_b, l, b_out[l], ln1_g[l], ln1_b[l], w_router, b_router)
        expert_ids = routed[:, 0:2].astype(jnp.int32)
        row_token, row_dest, tile_expert, n_used, run_start, next_expert = dispatch_plan(expert_ids, n_steps)
        base = l * N_EXPERTS
        y_assign = expert_mlps(h1, wg_all, wu_all, wd_all, tile_expert + base, n_used, row_token, row_dest,
                               run_start, jnp.where(next_expert >= 0, next_expert + base, -1))
        h, hb = combine_ln(alpha, y_assign, h1, routed, ln2_g[l], ln2_b[l])
    return h.reshape(bsz, seq, d)[:, N_META:]
```

```python
import functools
import math

import jax
import jax.numpy as jnp
import numpy as np
from jax import lax
from jax.experimental import pallas as pl
from jax.experimental.pallas import tpu as pltpu

F32 = jnp.float32
BF16 = jnp.bfloat16
HIGHEST = lax.Precision.HIGHEST

D_MODEL = 2048
N_META = 16
GLA_HEADS = 4
GLA_DK = 64
GLA_DV = 128
GLA_WIDTH = GLA_HEADS * GLA_DV
GLA_RANK = 16
GLA_TAU = 16.0
GLA_CHUNK = 64
GLA_UNROLL = 11
SWA_HEADS = 8
SWA_KV_HEADS = 2
SWA_GROUP = SWA_HEADS // SWA_KV_HEADS
SWA_HEAD_DIM = 128
SWA_WIDTH = SWA_HEADS * SWA_HEAD_DIM
WINDOW = 128
HY_WIDTH = 512
HY_BANDS = 16
HY_EMB = 2 * HY_BANDS + 1
HY_FFN = 64
HY_FAST_PCT = 0.3
HY_SLOW_PCT = 1.5
HY_TARGET = 1e-2
N_GROUPS = 4
EXPERTS_PER_GROUP = 4
N_EXPERTS = N_GROUPS * EXPERTS_PER_GROUP
D_EXPERT = 1024
LN_EPS = 1e-5
RMS_EPS = 1e-6
NEG = -1e30

LANES = 128
A_COLS = 1664
GATE_COL0 = 1536
ROW_TILE = 384
FUSED_TILE = 192
MOE_TILE = 256
VMEM_LIMIT = 56 * 1024 * 1024


def _cparams(sem):
    return pltpu.CompilerParams(dimension_semantics=sem, vmem_limit_bytes=VMEM_LIMIT)


def _resident(shape, index_map):
    return pl.BlockSpec(shape, index_map, pipeline_mode=pl.Buffered(1))


def _dot(a, b, **kw):
    return jnp.dot(a, b, preferred_element_type=F32, **kw)


def _dot_nt(a, b):
    return lax.dot_general(a, b, (((1,), (1,)), ((), ())), preferred_element_type=F32)


def _ln(x, g, b):
    mu = jnp.mean(x, axis=-1, keepdims=True)
    xc = x - mu
    var = jnp.mean(xc * xc, axis=-1, keepdims=True)
    return xc * lax.rsqrt(var + LN_EPS) * g + b


def _ln_rows_kernel(x_ref, g_ref, b_ref, o_ref, ob_ref):
    y = _ln(x_ref[...], g_ref[...], b_ref[...])
    o_ref[...] = y
    ob_ref[...] = y.astype(BF16)


def ln_rows(x, g, b):
    t, d = x.shape
    row = pl.BlockSpec((ROW_TILE, d), lambda i: (i, 0))
    vec = pl.BlockSpec((1, d), lambda i: (0, 0))
    return pl.pallas_call(
        _ln_rows_kernel,
        grid=(t // ROW_TILE,),
        in_specs=[row, vec, vec],
        out_specs=[row, row],
        out_shape=[jax.ShapeDtypeStruct((t, d), F32), jax.ShapeDtypeStruct((t, d), BF16)],
        compiler_params=_cparams(("parallel",)),
        name="ln_rows",
    )(x, g.reshape(1, d), b.reshape(1, d))


def _proj_kernel(x_ref, w_ref, b_ref, o_ref):
    o_ref[...] = (_dot(x_ref[...], w_ref[0]) + b_ref[...]).astype(o_ref.dtype)


def project(x, w, layer, b, out_dtype):
    t, k = x.shape
    n = w.shape[2]
    return pl.pallas_call(
        _proj_kernel,
        grid=(t // ROW_TILE,),
        in_specs=[pl.BlockSpec((ROW_TILE, k), lambda i: (i, 0)),
                  _resident((1, k, n), lambda i: (layer, 0, 0)),
                  pl.BlockSpec((1, n), lambda i: (0, 0))],
        out_specs=pl.BlockSpec((ROW_TILE, n), lambda i: (i, 0)),
        out_shape=jax.ShapeDtypeStruct((t, n), out_dtype),
        compiler_params=_cparams(("parallel",)),
        name="in_proj",
    )(x, w, b)


def _log_sigmoid(x):
    return jnp.minimum(x, 0.0) - jnp.log(1.0 + jnp.exp(-jnp.abs(x)))


def _chunk_scan(x, reverse):
    c = x.shape[0]
    idx = lax.broadcasted_iota(jnp.int32, x.shape, 0)
    s = 1
    while s < c:
        if reverse:
            x = x + jnp.where(idx < c - s, pltpu.roll(x, c - s, axis=0), 0.0)
        else:
            x = x + jnp.where(idx >= s, pltpu.roll(x, s, axis=0), 0.0)
        s *= 2
    return x


def _gla_kernel(q_ref, k_ref, v_ref, r_ref, gt_ref, w2f_ref, bf_ref, w2b_ref, bb_ref, ng_ref, o_ref,
                qs, ks, vs, lfs, lbs, acc, acc_b, st):
    seq = q_ref.shape[1]
    seq_pad = qs.shape[0]
    n_chunks = seq_pad // GLA_CHUNK
    tail = seq_pad - seq
    c = GLA_CHUNK

    gates = gt_ref[0]
    log_f = _log_sigmoid(_dot(gates, w2f_ref[...], precision=HIGHEST) + bf_ref[...]) * (1.0 / GLA_TAU)
    log_b = _log_sigmoid(_dot(gates, w2b_ref[...], precision=HIGHEST) + bb_ref[...]) * (1.0 / GLA_TAU)
    for dst, src in ((qs, q_ref[0]), (ks, k_ref[0]), (vs, v_ref[0]), (lfs, log_f), (lbs, log_b)):
        dst[0:seq, :] = src
        dst[seq:seq_pad, :] = jnp.zeros((tail, dst.shape[1]), F32)

    lane = lax.broadcasted_iota(jnp.int32, (c, 2 * GLA_DK), 1)
    row = lax.broadcasted_iota(jnp.int32, (c, c), 0)
    col = lax.broadcasted_iota(jnp.int32, (c, c), 1)
    scale = GLA_DK ** -0.5

    st[...] = jnp.zeros(st.shape, F32)

    def chunk(n, log_ref, reverse, out_ref):
        keep = (col > row) if reverse else (col <= row)
        r0 = pl.multiple_of(n * c, c)
        cum = _chunk_scan(log_ref[pl.ds(r0, c), :], reverse)
        tot = cum[0:1, :] if reverse else cum[c - 1:c, :]
        q = qs[pl.ds(r0, c), :]
        k = ks[pl.ds(r0, c), :]
        q_dec = q * jnp.exp(cum) * scale
        k_inv = (k * jnp.exp(-cum)).astype(BF16)
        k_dec = (k * jnp.exp(tot - cum)).astype(BF16)
        decay = jnp.exp(tot)
        for h in range(2):
            slot = 2 * int(reverse) + h
            head_lanes = (lane < GLA_DK) if h == 0 else (lane >= GLA_DK)
            qm = jnp.where(head_lanes, q_dec, 0.0).astype(BF16)
            att = jnp.where(keep, _dot_nt(qm, k_inv), 0.0)
            vh = vs[pl.ds(r0, c), h * GLA_DV:(h + 1) * GLA_DV]
            state = st[slot]
            out_ref[pl.ds(r0, c), h * GLA_DV:(h + 1) * GLA_DV] = (
                _dot(att.astype(BF16), vh.astype(BF16)) + _dot_nt(qm, state.astype(BF16)))
            st[slot] = state * decay + _dot(vh.T.astype(BF16), k_dec)

    def body(i, carry):
        chunk(i, lfs, False, acc)
        chunk(n_chunks - 1 - i, lbs, True, acc_b)
        return carry

    lax.fori_loop(0, n_chunks, body, 0, unroll=GLA_UNROLL)

    r = r_ref[0]
    gate = r * (1.0 / (1.0 + jnp.exp(-r)))
    for h in range(2):
        sl = slice(h * GLA_DV, (h + 1) * GLA_DV)
        o = acc[0:seq, sl] + acc_b[0:seq, sl]
        y = o * lax.rsqrt(jnp.mean(o * o, axis=-1, keepdims=True) + RMS_EPS) * ng_ref[:, sl]
        o_ref[0, :, sl] = (y * gate[:, sl]).astype(o_ref.dtype)


def gla_branch(u_a, w2f, b_f, w2b, b_b, norm_g):
    bsz, seq, _ = u_a.shape
    seq_pad = -(-seq // GLA_CHUNK) * GLA_CHUNK
    pair = 2 * GLA_DK

    def col(width, first_block):
        return pl.BlockSpec((1, seq, width), lambda b, p: (b, 0, first_block + p))

    vec128 = pl.BlockSpec((1, pair), lambda b, p: (0, p))
    return pl.pallas_call(
        _gla_kernel,
        grid=(bsz, GLA_HEADS // 2),
        in_specs=[col(pair, 0), col(pair, 2), col(2 * GLA_DV, 2), col(2 * GLA_DV, 4),
                  pl.BlockSpec((1, seq, LANES), lambda b, p: (b, 0, GATE_COL0 // LANES)),
                  pl.BlockSpec((LANES, pair), lambda b, p: (0, p)), vec128,
                  pl.BlockSpec((LANES, pair), lambda b, p: (0, p)), vec128,
                  pl.BlockSpec((1, 2 * GLA_DV), lambda b, p: (0, p))],
        out_specs=pl.BlockSpec((1, seq, 2 * GLA_DV), lambda b, p: (b, 0, p)),
        out_shape=jax.ShapeDtypeStruct((bsz, seq, GLA_WIDTH), BF16),
        scratch_shapes=[pltpu.VMEM((seq_pad, pair), F32), pltpu.VMEM((seq_pad, pair), F32),
                        pltpu.VMEM((seq_pad, 2 * GLA_DV), F32), pltpu.VMEM((seq_pad, pair), F32),
                        pltpu.VMEM((seq_pad, pair), F32), pltpu.VMEM((seq_pad, 2 * GLA_DV), F32),
                        pltpu.VMEM((seq_pad, 2 * GLA_DV), F32), pltpu.VMEM((4, GLA_DV, pair), F32)],
        compiler_params=_cparams(("parallel", "parallel")),
        name="gla",
    )(u_a, u_a, u_a, u_a, u_a, w2f, b_f.reshape(1, -1), w2b, b_b.reshape(1, -1), norm_g.reshape(1, -1))


LOG2E = 1.4426950408889634
SWA_Q_SCALE = SWA_HEAD_DIM ** -0.5 * LOG2E


def _swa_slope2(h):
    return LOG2E * 2.0 ** (-8.0 * (h + 1) / SWA_HEADS)


def _swa_kernel(sink_ref, q_ref, k_ref, v_ref, ng_ref, o_ref, band_bias, meta_bias):
    seq = q_ref.shape[1]
    blk = WINDOW
    n_blocks = (seq - N_META) // blk
    hd = SWA_HEAD_DIM
    grp = SWA_GROUP

    def stacked(fn, nq):
        row = lax.broadcasted_iota(jnp.int32, (grp * nq, 1), 0)
        col = jnp.full((grp * nq, 1), fn(grp - 1), F32)
        for g in range(grp - 2, -1, -1):
            col = jnp.where(row < (g + 1) * nq, fn(g), col)
        return col

    def band_bias_tile(kv, nq, nk, delta):
        qi = lax.broadcasted_iota(jnp.int32, (grp * nq, nk), 0) & (nq - 1)
        ki = lax.broadcasted_iota(jnp.int32, (grp * nq, nk), 1)
        dist = jnp.abs(qi - ki + delta)
        slope = stacked(lambda g: _swa_slope2(kv * grp + g), nq)
        return jnp.where(dist <= WINDOW, -slope * dist.astype(F32), NEG)

    def meta_bias_tile(kv, nq, among_meta):
        qi = lax.broadcasted_iota(jnp.int32, (grp * nq, blk), 0) & (nq - 1)
        ki = lax.broadcasted_iota(jnp.int32, (grp * nq, blk), 1)
        rel = jnp.abs(qi - ki) if among_meta else qi - ki
        slope = stacked(lambda g: _swa_slope2(kv * grp + g), nq)
        return jnp.where(ki < N_META, -slope * rel.astype(F32), NEG)

    def attend(r0, nq, w0, nk, band_fn, meta_fn):
        outs = [None] * SWA_HEADS
        for kv in range(SWA_KV_HEADS):
            ks = slice(kv * hd, (kv + 1) * hd)
            q4 = jnp.concatenate([q_ref[0, pl.ds(r0, nq), (kv * grp + g) * hd:(kv * grp + g + 1) * hd]
                                  for g in range(grp)], axis=0)
            s_band = _dot_nt(q4, k_ref[0, pl.ds(w0, nk), ks]) + band_fn(kv)
            s_meta = _dot_nt(q4, k_ref[0, 0:blk, ks]) + meta_fn(kv)
            sink = stacked(lambda g: sink_ref[kv * grp + g] * LOG2E, nq)
            m = jnp.maximum(jnp.maximum(jnp.max(s_band, axis=-1, keepdims=True),
                                        jnp.max(s_meta, axis=-1, keepdims=True)), sink)
            p_band = jnp.exp2(s_band - m)
            p_meta = jnp.exp2(s_meta - m)
            den = (jnp.sum(p_band, axis=-1, keepdims=True) + jnp.sum(p_meta, axis=-1, keepdims=True)
                   + jnp.exp2(sink - m))
            o4 = (_dot(p_band.astype(BF16), v_ref[0, pl.ds(w0, nk), ks])
                  + _dot(p_meta.astype(BF16), v_ref[0, 0:blk, ks])) * (1.0 / den)
            for g in range(grp):
                outs[kv * grp + g] = o4[g * nq:(g + 1) * nq]
        ssq = outs[0] * outs[0]
        for o in outs[1:]:
            ssq = ssq + o * o
        inv = lax.rsqrt(jnp.sum(ssq, axis=-1, keepdims=True) * (1.0 / SWA_WIDTH) + RMS_EPS)
        for h in range(SWA_HEADS):
            cols = slice(h * hd, (h + 1) * hd)
            o_ref[0, pl.ds(r0, nq), cols] = (outs[h] * inv * ng_ref[:, cols]).astype(o_ref.dtype)

    for kv in range(SWA_KV_HEADS):
        band_bias[kv] = band_bias_tile(kv, blk, 3 * blk, blk)
        meta_bias[kv] = meta_bias_tile(kv, blk, False)

    def meta_fn_at(r0):
        def fn(kv):
            slope = stacked(lambda g: _swa_slope2(kv * grp + g), blk)
            return meta_bias[kv] - slope * r0
        return fn

    last_r0 = N_META + (n_blocks - 1) * blk
    attend(0, N_META, N_META, blk, lambda kv: band_bias_tile(kv, N_META, blk, -N_META),
           lambda kv: meta_bias_tile(kv, N_META, True))
    attend(N_META, blk, N_META, 3 * blk, lambda kv: band_bias_tile(kv, blk, 3 * blk, 0), meta_fn_at(float(N_META)))
    attend(last_r0, blk, last_r0 - 2 * blk, 3 * blk, lambda kv: band_bias_tile(kv, blk, 3 * blk, 2 * blk),
           meta_fn_at(float(last_r0)))

    def body(j, carry):
        r0 = pl.multiple_of(N_META + j * blk, N_META)
        attend(r0, blk, pl.multiple_of(r0 - blk, N_META), 3 * blk, lambda kv: band_bias[kv],
               meta_fn_at(r0.astype(F32)))
        return carry

    lax.fori_loop(1, n_blocks - 1, body, 0)


def swa_branch(u_b, sink, norm_g):
    bsz, seq, _ = u_b.shape
    kvw = SWA_KV_HEADS * SWA_HEAD_DIM
    rows = SWA_GROUP * WINDOW
    return pl.pallas_call(
        _swa_kernel,
        grid=(bsz,),
        in_specs=[pl.BlockSpec(memory_space=pltpu.SMEM),
                  pl.BlockSpec((1, seq, SWA_WIDTH), lambda b: (b, 0, 0)),
                  pl.BlockSpec((1, seq, kvw), lambda b: (b, 0, SWA_WIDTH // kvw)),
                  pl.BlockSpec((1, seq, kvw), lambda b: (b, 0, SWA_WIDTH // kvw + 1)),
                  pl.BlockSpec((1, SWA_WIDTH), lambda b: (0, 0))],
        out_specs=pl.BlockSpec((1, seq, SWA_WIDTH), lambda b: (b, 0, 0)),
        out_shape=jax.ShapeDtypeStruct((bsz, seq, SWA_WIDTH), BF16),
        scratch_shapes=[pltpu.VMEM((SWA_KV_HEADS, rows, 3 * WINDOW), F32),
                        pltpu.VMEM((SWA_KV_HEADS, rows, WINDOW), F32)],
        compiler_params=_cparams(("parallel",)),
        name="swa",
    )(sink, u_b, u_b, u_b, norm_g.reshape(1, -1))


@functools.lru_cache(maxsize=None)
def _hyena_tables(seq):
    t = np.linspace(0.0, 1.0, seq, dtype=np.float32)[:, None]
    w = (np.float32(2.0 * math.pi) * np.arange(seq, dtype=np.float32)[:, None] / np.float32(seq)).astype(np.float32)
    bands = np.linspace(1e-4, HY_BANDS - 1, HY_BANDS, dtype=np.float32)
    arg = (bands * w).astype(np.float32).astype(np.float64)
    z = np.concatenate([t.astype(np.float64), np.cos(arg), -np.sin(arg)], axis=-1)
    z_pad = np.zeros((seq, LANES), np.float32)
    z_pad[:, :HY_EMB] = z.astype(np.float32)
    max_decay = math.log(HY_TARGET) / HY_FAST_PCT
    min_decay = math.log(HY_TARGET) / HY_SLOW_PCT
    rates = np.abs(np.linspace(min_decay, max_decay, HY_WIDTH, dtype=np.float32))[None, :]
    n = 2 * seq - 1
    fj = (np.arange(seq, dtype=np.int64)[:, None] * np.arange(seq, dtype=np.int64)[None, :]) % n
    ang = fj.astype(np.float64) * (2.0 * math.pi / n)
    return z_pad, rates.astype(np.float32), np.cos(ang).astype(np.float32), np.sin(ang).astype(np.float32)


def _hy_filter_kernel(z_ref, rate_ref, w1_ref, b1_ref, fr_ref, w2_ref, b2_ref, w3_ref, hs_ref, hd_ref):
    z = z_ref[...]
    h = jnp.sin(fr_ref[0, 0:1, :] * (_dot(z, w1_ref[0], precision=HIGHEST) + b1_ref[0]))
    h = jnp.sin(fr_ref[0, 1:2, :] * (_dot(h, w2_ref[0], precision=HIGHEST) + b2_ref[0]))
    h = _dot(h, w3_ref[0], precision=HIGHEST)
    window = jnp.exp(-z[:, 0:1] * rate_ref[...])
    h_fwd = h[:, :HY_WIDTH] * window
    h_bwd = h[:, HY_WIDTH:] * window
    first = lax.broadcasted_iota(jnp.int32, h_bwd.shape, 0) == 0
    h_bwd = jnp.where(first, 0.0, h_bwd)
    hs_ref[0] = h_fwd + h_bwd
    hd_ref[0] = h_bwd - h_fwd


def hyena_filters(z, rates, w1p, b1, freq, w2, b2, w3):
    depth = w1p.shape[0]
    seq = z.shape[0]

    def per_layer(*shape):
        return pl.BlockSpec((1,) + shape, lambda l: (l,) + (0,) * len(shape))

    out = jax.ShapeDtypeStruct((depth, seq, HY_WIDTH), F32)
    return pl.pallas_call(
        _hy_filter_kernel,
        grid=(depth,),
        in_specs=[pl.BlockSpec((seq, LANES), lambda l: (0, 0)), pl.BlockSpec((1, HY_WIDTH), lambda l: (0, 0)),
                  per_layer(LANES, HY_FFN), per_layer(1, HY_FFN), per_layer(2, HY_FFN),
                  per_layer(HY_FFN, HY_FFN), per_layer(1, HY_FFN), per_layer(HY_FFN, 2 * HY_WIDTH)],
        out_specs=[per_layer(seq, HY_WIDTH), per_layer(seq, HY_WIDTH)],
        out_shape=[out, out],
        compiler_params=_cparams(("parallel",)),
        name="hyena_filters",
    )(z, rates, w1p, b1, freq, w2, b2, w3)


def _split_bf16(x):
    hi = x.astype(BF16)
    return hi, (x - hi.astype(F32)).astype(BF16)


def _hy_spectrum_kernel(cos_ref, sin_ref, hs_ref, hd_ref, kc_ref, ks_ref):
    seq = cos_ref.shape[0]
    n = 2 * seq - 1
    f = lax.broadcasted_iota(jnp.int32, (seq, 1), 0)
    weight = jnp.where(f == 0, 1.0 / n, 2.0 / n)
    s_hi, s_lo = _split_bf16(hs_ref[0])
    d_hi, d_lo = _split_bf16(hd_ref[0])
    kc_ref[0] = (_dot(cos_ref[...], s_hi) + _dot(cos_ref[...], s_lo)) * weight
    ks_ref[0] = (_dot(sin_ref[...], d_hi) + _dot(sin_ref[...], d_lo)) * weight


def hyena_spectrum(cos_t, sin_t, h_sum, h_diff):
    depth, seq, _ = h_sum.shape
    half = HY_WIDTH // 2
    table = _resident((seq, seq), lambda l, c: (0, 0))
    blk = pl.BlockSpec((1, seq, half), lambda l, c: (l, 0, c))
    out = jax.ShapeDtypeStruct((depth, seq, HY_WIDTH), F32)
    return pl.pallas_call(
        _hy_spectrum_kernel,
        grid=(depth, 2),
        in_specs=[table, table, blk, blk],
        out_specs=[blk, blk],
        out_shape=[out, out],
        compiler_params=_cparams(("parallel", "parallel")),
        name="hyena_spectrum",
    )(cos_t, sin_t, h_sum, h_diff)


def _short_conv(u, w, b):
    seq = u.shape[0]
    row = lax.broadcasted_iota(jnp.int32, u.shape, 0)
    prev = jnp.where(row == 0, 0.0, pltpu.roll(u, 1, axis=0))
    nxt = jnp.where(row == seq - 1, 0.0, pltpu.roll(u, seq - 1, axis=0))
    return prev * w[0:1, :] + u * w[1:2, :] + nxt * w[2:3, :] + b


def _hyena_kernel(x0_ref, x1_ref, v_ref, w0_ref, w1_ref, wv_ref, b0_ref, b1_ref, bv_ref,
                  cos_ref, sin_ref, kc_ref, ks_ref, skip_ref, ng_ref, o_ref, acc):
    c = pl.program_id(1)
    half = x0_ref.shape[2]
    x0 = _short_conv(x0_ref[0], w0_ref[...], b0_ref[...])
    x1 = _short_conv(x1_ref[0], w1_ref[...], b1_ref[...])
    v = _short_conv(v_ref[0], wv_ref[...], bv_ref[...])
    g = x1 * v
    gb = g.astype(BF16)
    uc = _dot(cos_ref[...], gb)
    us = _dot(sin_ref[...], gb)
    kc = kc_ref[0]
    ks = ks_ref[0]
    pr = (uc * kc + us * ks).astype(BF16)
    pi = (uc * ks - us * kc).astype(BF16)
    y = _dot(cos_ref[...], pr) - _dot(sin_ref[...], pi) + g * skip_ref[...]
    y = x0 * y

    @pl.when(c == 0)
    def _():
        acc[:, 0:half] = y

    @pl.when(c == 1)
    def _():
        acc[:, half:2 * half] = y
        full = acc[...]
        o_ref[0] = (full * lax.rsqrt(jnp.mean(full * full, axis=-1, keepdims=True) + RMS_EPS)
                    * ng_ref[...]).astype(o_ref.dtype)


def hyena_branch(u_c, conv_w, conv_b, cos_t, sin_t, k_cos, k_sin, layer, skip, norm_g):
    bsz, seq, _ = u_c.shape
    half = HY_WIDTH // 2

    def stream(first_block):
        return pl.BlockSpec((1, seq, half), lambda b, c: (b, 0, first_block + c))

    def wcol(rows, first_block):
        return pl.BlockSpec((rows, half), lambda b, c: (0, first_block + c))

    table = _resident((seq, seq), lambda b, c: (0, 0))
    spec = pl.BlockSpec((1, seq, half), lambda b, c: (layer, 0, c))
    return pl.pallas_call(
        _hyena_kernel,
        grid=(bsz, 2),
        in_specs=[stream(0), stream(2), stream(4), wcol(3, 0), wcol(3, 2), wcol(3, 4),
                  wcol(1, 0), wcol(1, 2), wcol(1, 4), table, table, spec, spec, wcol(1, 0),
                  pl.BlockSpec((1, HY_WIDTH), lambda b, c: (0, 0))],
        out_specs=pl.BlockSpec((1, seq, HY_WIDTH), lambda b, c: (b, 0, 0)),
        out_shape=jax.ShapeDtypeStruct((bsz, seq, HY_WIDTH), BF16),
        scratch_shapes=[pltpu.VMEM((seq, HY_WIDTH), F32)],
        compiler_params=_cparams(("parallel", "arbitrary")),
        name="hyena",
    )(u_c, u_c, u_c, conv_w, conv_w, conv_w, conv_b, conv_b, conv_b, cos_t, sin_t, k_cos, k_sin,
      skip.reshape(1, -1), norm_g.reshape(1, -1))


def _first_index_of_max(x, valid, lane):
    m = jnp.max(jnp.where(valid, x, NEG), axis=-1, keepdims=True)
    idx = jnp.min(jnp.where(valid & (x == m), lane, float(LANES)), axis=-1, keepdims=True)
    return m, idx


def _route(x):
    lane = lax.broadcasted_iota(jnp.int32, x.shape, 1).astype(F32)
    is_group = lane < N_GROUPS
    gm, g_idx = _first_index_of_max(x, is_group, lane)
    g_top = 1.0 / jnp.sum(jnp.where(is_group, jnp.exp(x - gm), 0.0), axis=-1, keepdims=True)
    lo = N_GROUPS + g_idx * EXPERTS_PER_GROUP
    in_group = (lane >= lo) & (lane < lo + EXPERTS_PER_GROUP)
    m1, i1 = _first_index_of_max(x, in_group, lane)
    m2, i2 = _first_index_of_max(x, in_group & (lane != i1), lane)
    e2 = jnp.exp(m2 - m1)
    w1 = g_top / (1.0 + e2)
    w2 = g_top * e2 / (1.0 + e2)
    return jnp.where(lane == 0, i1 - N_GROUPS,
                     jnp.where(lane == 1, i2 - N_GROUPS,
                               jnp.where(lane == 2, w1, jnp.where(lane == 3, w2, 0.0))))


def _outproj_kernel(alpha, ya_ref, yb_ref, yc_ref, h_ref, w_ref, b_ref, g_ref, be_ref, wr_ref, br_ref,
                    h1_ref, rt_ref, z_scr):
    i = pl.program_id(0)
    last = pl.num_programs(0) - 1
    na = ya_ref.shape[1]
    nb = yb_ref.shape[1]

    def project(slot):
        mix = (_dot(ya_ref[...], w_ref[0, 0:na, :]) + _dot(yb_ref[...], w_ref[0, na:na + nb, :])
               + _dot(yc_ref[...], w_ref[0, na + nb:, :]) + b_ref[...])
        z_scr[slot] = alpha * h_ref[...] + mix

    def finish(slot):
        h1 = _ln(z_scr[slot], g_ref[...], be_ref[...])
        h1_ref[...] = h1
        rt_ref[...] = _route(_dot(h1.astype(BF16), wr_ref[0]) + br_ref[0])

    @pl.when(i == 0)
    def _():
        project(0)

    for cur in range(2):
        @pl.when((i % 2 == cur) & (i > 0) & (i < last))
        def _():
            finish(1 - cur)
            project(cur)

        @pl.when((i % 2 == cur) & (i == last))
        def _():
            finish(1 - cur)


def out_projection(alpha, y_a, y_b, y_c, h, w, layer, b, g, be, w_router, b_router):
    t, d = h.shape
    n_tiles = t // ROW_TILE

    def row_in(width):
        return pl.BlockSpec((ROW_TILE, width), lambda i: (jnp.minimum(i, n_tiles - 1), 0))

    def row_out(width):
        return pl.BlockSpec((ROW_TILE, width), lambda i: (jnp.maximum(i - 1, 0), 0))

    def vec():
        return pl.BlockSpec((1, d), lambda i: (0, 0))

    return pl.pallas_call(
        functools.partial(_outproj_kernel, alpha),
        grid=(n_tiles + 1,),
        in_specs=[row_in(y_a.shape[1]), row_in(y_b.shape[1]), row_in(y_c.shape[1]), row_in(d),
                  _resident((1, d, d), lambda i: (layer, 0, 0)), vec(), vec(), vec(),
                  _resident((1, d, LANES), lambda i: (layer, 0, 0)),
                  pl.BlockSpec((1, 1, LANES), lambda i: (layer, 0, 0))],
        out_specs=[row_out(d), row_out(LANES)],
        out_shape=[jax.ShapeDtypeStruct((t, d), F32), jax.ShapeDtypeStruct((t, LANES), F32)],
        scratch_shapes=[pltpu.VMEM((2, ROW_TILE, d), F32)],
        compiler_params=_cparams(("arbitrary",)),
        name="out_proj_ln",
    )(y_a, y_b, y_c, h, w, b.reshape(1, d), g.reshape(1, d), be.reshape(1, d), w_router, b_router)


def dispatch_plan(expert_ids, n_steps):
    t = expert_ids.shape[0]
    tm = MOE_TILE
    flat = expert_ids.reshape(-1)
    experts = jnp.arange(N_EXPERTS, dtype=jnp.int32)
    counts = jnp.sum((flat[:, None] == experts[None, :]).astype(jnp.int32), axis=0)
    padded = (counts + tm - 1) // tm * tm
    ends = jnp.cumsum(padded)
    pad_needed = jnp.arange(tm, dtype=jnp.int32)[None, :] < (padded - counts)[:, None]
    pad_keys = jnp.where(pad_needed, 2 * experts[:, None] + 1, 2 * N_EXPERTS).reshape(-1)
    keys = jnp.concatenate([2 * flat, pad_keys])
    vals = jnp.concatenate([jnp.arange(2 * t, dtype=jnp.int32), jnp.full((N_EXPERTS * tm,), 2 * t, jnp.int32)])
    _, ordered = lax.sort((keys, vals), num_keys=1)
    ordered = jnp.pad(ordered, (tm, (n_steps + 1) * tm - tm - ordered.shape[0]), constant_values=2 * t)
    lane = jnp.arange(ordered.shape[0], dtype=jnp.int32) % tm
    row_assign = jnp.where(ordered >= 2 * t, 2 * t + lane, ordered)
    plane = t + MOE_TILE // 2
    row_token = jnp.minimum(row_assign >> 1, t - 1)
    row_dest = (row_assign >> 1) + (row_assign & 1) * plane
    n_used = (ends[-1] // MOE_TILE).astype(jnp.int32)
    step = jnp.arange(n_steps, dtype=jnp.int32)
    tile_start = jnp.minimum(step, n_used - 1) * MOE_TILE
    tile_expert = jnp.sum((tile_start[:, None] >= ends[None, :]).astype(jnp.int32), axis=1)
    run_start = (step < n_used) & ((step == 0) | (tile_expert != jnp.roll(tile_expert, 1)))
    after = ends[tile_expert] // MOE_TILE
    next_expert = jnp.where(after < n_used, tile_expert[jnp.minimum(after, n_steps - 1)], -1)
    return (row_token, row_dest, tile_expert.astype(jnp.int32), n_used.reshape(1), run_start.astype(jnp.int32),
            next_expert.astype(jnp.int32))


def _moe_kernel(te_ref, nu_ref, rt_ref, rd_ref, rs_ref, ne_ref, x_hbm, wg_hbm, wu_hbm, wd_hbm, y_hbm,
                xbuf, ybuf, land_g, land_u, land_d, wg_ref, wu_ref, wd_ref, gsem, ssem, wsem):
    i = pl.program_id(0)
    n_used = nu_ref[0]
    tm = xbuf.shape[1]

    def weight_copies(e):
        return (pltpu.make_async_copy(wg_hbm.at[e], land_g, wsem.at[0]),
                pltpu.make_async_copy(wu_hbm.at[e], land_u, wsem.at[1]),
                pltpu.make_async_copy(wd_hbm.at[e], land_d, wsem.at[2]))

    @pl.when(i == 0)
    def _():
        for cp in weight_copies(te_ref[0]):
            cp.start()

    @pl.when(rs_ref[i] == 1)
    def _():
        for cp in weight_copies(te_ref[i]):
            cp.wait()
        wg_ref[...] = land_g[...].astype(BF16)
        wu_ref[...] = land_u[...].astype(BF16)
        wd_ref[...] = land_d[...].astype(BF16)

        @pl.when(ne_ref[i] >= 0)
        def _():
            for cp in weight_copies(ne_ref[i]):
                cp.start()

    def gather_copy(tile, slot, r):
        tok = rt_ref[(tile + 1) * tm + r]
        return pltpu.make_async_copy(x_hbm.at[pl.ds(tok, 1)], xbuf.at[slot, pl.ds(r, 1)], gsem.at[slot])

    def scatter_copy(tile, slot, r):
        dst = rd_ref[(tile + 1) * tm + r]
        return pltpu.make_async_copy(ybuf.at[slot, pl.ds(r, 1)], y_hbm.at[pl.ds(dst, 1)], ssem.at[slot])

    def wait_gather(slot):
        pltpu.make_async_copy(x_hbm.at[pl.ds(0, tm)], xbuf.at[slot], gsem.at[slot]).wait()

    def wait_scatter(slot):
        pltpu.make_async_copy(ybuf.at[slot], y_hbm.at[pl.ds(0, tm)], ssem.at[slot]).wait()

    def looped(copy_fn, tile, slot):
        def body(r, carry):
            copy_fn(tile, slot, r).start()
            return carry
        lax.fori_loop(0, tm, body, 0)

    @pl.when(i == 0)
    def _():
        ybuf[...] = jnp.zeros(ybuf.shape, ybuf.dtype)
        looped(gather_copy, 0, 0)

    for cur in range(2):
        nxt = 1 - cur

        @pl.when((i % 2 == cur) & (i <= n_used))
        def _():
            wait_gather(cur)

        @pl.when((i % 2 == cur) & (i >= 1) & (i <= n_used))
        def _():
            wait_scatter(cur)

        @pl.when((i % 2 == cur) & (i < n_used))
        def _():
            x = xbuf[cur].astype(BF16)

            def neighbour_copies(lo, hi):
                for r in range(lo, hi):
                    gather_copy(i + 1, nxt, r).start()
                    scatter_copy(i - 1, nxt, r).start()

            quarter = tm // 4
            neighbour_copies(0, quarter)
            a = _dot(x, wg_ref[...])
            neighbour_copies(quarter, 2 * quarter)
            u = _dot(x, wu_ref[...])
            neighbour_copies(2 * quarter, 3 * quarter)
            hmid = (a * (1.0 / (1.0 + jnp.exp(-a))) * u).astype(BF16)
            neighbour_copies(3 * quarter, tm)
            ybuf[cur] = _dot(hmid, wd_ref[...])

        @pl.when((i % 2 == cur) & (i == n_used))
        def _():
            looped(scatter_copy, i - 1, nxt)
            wait_scatter(nxt)


def expert_mlps(x, w_gate, w_up, w_down, tile_expert, n_used, row_token, row_dest, run_start, next_expert):
    t, d = x.shape
    n_steps = tile_expert.shape[0]
    f = w_gate.shape[2]
    hbm = pl.BlockSpec(memory_space=pl.ANY)
    grid_spec = pltpu.PrefetchScalarGridSpec(
        num_scalar_prefetch=6,
        grid=(n_steps,),
        in_specs=[hbm, hbm, hbm, hbm],
        out_specs=hbm,
        scratch_shapes=[pltpu.VMEM((2, MOE_TILE, d), F32), pltpu.VMEM((2, MOE_TILE, d), F32),
                        pltpu.VMEM((d, f), F32), pltpu.VMEM((d, f), F32), pltpu.VMEM((f, d), F32),
                        pltpu.VMEM((d, f), BF16), pltpu.VMEM((d, f), BF16), pltpu.VMEM((f, d), BF16),
                        pltpu.SemaphoreType.DMA((2,)), pltpu.SemaphoreType.DMA((2,)), pltpu.SemaphoreType.DMA((3,))],
    )
    return pl.pallas_call(
        _moe_kernel,
        grid_spec=grid_spec,
        out_shape=jax.ShapeDtypeStruct((2 * (t + MOE_TILE // 2), d), F32),
        compiler_params=_cparams(("arbitrary",)),
        name="moe_experts",
    )(tile_expert, n_used, row_token, row_dest, run_start, next_expert, x, w_gate, w_up, w_down
      ).reshape(2, t + MOE_TILE // 2, d)


def _combine_kernel(alpha, y1_ref, y2_ref, h_ref, rw_ref, g_ref, b_ref, o_ref, ob_ref):
    rw = rw_ref[...]
    moe = y1_ref[0] * rw[:, 2:3] + y2_ref[0] * rw[:, 3:4]
    y = _ln(alpha * h_ref[...] + moe, g_ref[...], b_ref[...])
    o_ref[...] = y
    ob_ref[...] = y.astype(BF16)


def combine_ln(alpha, y_assign, h, route_out, g, b):
    t, d = h.shape
    tm = ROW_TILE
    row = pl.BlockSpec((tm, d), lambda i: (i, 0))
    vec = pl.BlockSpec((1, d), lambda i: (0, 0))
    return pl.pallas_call(
        functools.partial(_combine_kernel, alpha),
        grid=(t // tm,),
        in_specs=[pl.BlockSpec((1, tm, d), lambda i: (0, i, 0)), pl.BlockSpec((1, tm, d), lambda i: (1, i, 0)), row,
                  pl.BlockSpec((tm, LANES), lambda i: (i, 0)), vec, vec],
        out_specs=[row, row],
        out_shape=[jax.ShapeDtypeStruct((t, d), F32), jax.ShapeDtypeStruct((t, d), BF16)],
        compiler_params=_cparams(("parallel",)),
        name="moe_combine_ln",
    )(y_assign, y_assign, h, route_out, g.reshape(1, d), b.reshape(1, d))


def _combine_project_kernel(alpha, y1_ref, y2_ref, h_ref, rw_ref, g_ref, b_ref, wa_ref, wb_ref, wc_ref,
                            ba_ref, bb_ref, bc_ref, o_ref, ua_ref, ub_ref, uc_ref, hb_scr):
    i = pl.program_id(0)
    last = pl.num_programs(0) - 1

    def combine(slot):
        rw = rw_ref[...]
        moe = y1_ref[0] * rw[:, 2:3] + y2_ref[0] * rw[:, 3:4]
        y = _ln(alpha * h_ref[...] + moe, g_ref[...], b_ref[...])
        o_ref[...] = y
        hb_scr[slot] = y.astype(BF16)

    def project(slot):
        x = hb_scr[slot]
        ua_ref[...] = _dot(x, wa_ref[0]) + ba_ref[...]
        ub_ref[...] = (_dot(x, wb_ref[0]) + bb_ref[...]).astype(ub_ref.dtype)
        uc_ref[...] = _dot(x, wc_ref[0]) + bc_ref[...]

    @pl.when(i == 0)
    def _():
        combine(0)

    for cur in range(2):
        @pl.when((i % 2 == cur) & (i > 0) & (i < last))
        def _():
            combine(cur)
            project(1 - cur)

        @pl.when((i % 2 == cur) & (i == last))
        def _():
            project(1 - cur)


def combine_ln_project(alpha, y_assign, h, route_out, g, b, w_a, w_b, w_c, layer, b_a, b_b, b_c):
    t, d = h.shape
    tm = FUSED_TILE
    n_tiles = t // tm

    def cur(i):
        return jnp.minimum(i, n_tiles - 1)

    def prev(i):
        return jnp.maximum(i - 1, 0)

    def weight(w):
        return _resident((1, d, w.shape[2]), lambda i: (layer, 0, 0))

    def bias(n):
        return pl.BlockSpec((1, n), lambda i: (0, 0))

    na, nb, nc = w_a.shape[2], w_b.shape[2], w_c.shape[2]
    vec = pl.BlockSpec((1, d), lambda i: (0, 0))
    return pl.pallas_call(
        functools.partial(_combine_project_kernel, alpha),
        grid=(n_tiles + 1,),
        in_specs=[pl.BlockSpec((1, tm, d), lambda i: (0, cur(i), 0)),
                  pl.BlockSpec((1, tm, d), lambda i: (1, cur(i), 0)),
                  pl.BlockSpec((tm, d), lambda i: (cur(i), 0)),
                  pl.BlockSpec((tm, LANES), lambda i: (cur(i), 0)), vec, vec,
                  weight(w_a), weight(w_b), weight(w_c), bias(na), bias(nb), bias(nc)],
        out_specs=[pl.BlockSpec((tm, d), lambda i: (cur(i), 0)),
                   pl.BlockSpec((tm, na), lambda i: (prev(i), 0)),
                   pl.BlockSpec((tm, nb), lambda i: (prev(i), 0)),
                   pl.BlockSpec((tm, nc), lambda i: (prev(i), 0))],
        out_shape=[jax.ShapeDtypeStruct((t, d), F32), jax.ShapeDtypeStruct((t, na), F32),
                   jax.ShapeDtypeStruct((t, nb), BF16), jax.ShapeDtypeStruct((t, nc), F32)],
        scratch_shapes=[pltpu.VMEM((2, tm, d), BF16)],
        compiler_params=_cparams(("arbitrary",)),
        name="combine_ln_in_proj",
    )(y_assign, y_assign, h, route_out, g.reshape(1, d), b.reshape(1, d), w_a, w_b, w_c, b_a, b_b, b_c)


def kernel(x, meta, emb_ln_g, emb_ln_b, w_in, b_in, gla_w2_f, gla_b_f, gla_w2_b, gla_b_b, gla_norm_g,
           swa_sink, swa_norm_g, hy_conv_w, hy_conv_b, hy_w1, hy_b1, hy_freq, hy_w2, hy_b2, hy_w3, hy_skip,
           hy_norm_g, w_out, b_out, ln1_g, ln1_b, router_wg, router_bg, router_we, router_be,
           exp_w_gate, exp_w_up, exp_w_down, ln2_g, ln2_b):
    bsz, seq_in, d = x.shape
    depth = w_in.shape[0]
    seq = seq_in + N_META
    t = bsz * seq
    alpha = (2.0 * depth) ** 0.25

    o_gates, o_r, o_swa, o_hy = 1024, 1056, 1568, 3104
    pad_cols = A_COLS - (o_swa)
    w_a = jnp.concatenate([w_in[:, :, :o_gates], w_in[:, :, o_r:o_swa], w_in[:, :, o_gates:o_r],
                           jnp.zeros((depth, d, pad_cols), w_in.dtype)], axis=-1).astype(BF16)
    b_a = jnp.concatenate([b_in[:, :o_gates], b_in[:, o_r:o_swa], b_in[:, o_gates:o_r],
                           jnp.zeros((depth, pad_cols), b_in.dtype)], axis=-1)
    q_scale = jnp.concatenate([jnp.full((SWA_WIDTH,), SWA_Q_SCALE, F32), jnp.ones((o_hy - o_swa - SWA_WIDTH,), F32)])
    w_b = (w_in[:, :, o_swa:o_hy] * q_scale).astype(BF16)
    b_b = b_in[:, o_swa:o_hy] * q_scale
    w_c = w_in[:, :, o_hy:].astype(BF16)
    w_out_b = w_out.astype(BF16)
    w2f = jnp.zeros((depth, LANES, gla_w2_f.shape[2]), F32).at[:, :GLA_RANK].set(gla_w2_f)
    w2b = jnp.zeros((depth, LANES, gla_w2_b.shape[2]), F32).at[:, GLA_RANK:2 * GLA_RANK].set(gla_w2_b)
    n_r = N_GROUPS + N_EXPERTS
    w_router = jnp.zeros((depth, d, LANES), F32).at[:, :, :N_GROUPS].set(router_wg)
    w_router = w_router.at[:, :, N_GROUPS:n_r].set(router_we).astype(BF16)
    b_router = jnp.zeros((depth, 1, LANES), F32).at[:, 0, :N_GROUPS].set(router_bg)
    b_router = b_router.at[:, 0, N_GROUPS:n_r].set(router_be)
    d_exp = exp_w_gate.shape[-1]
    wg_all = exp_w_gate.reshape(depth * N_EXPERTS, d, d_exp)
    wu_all = exp_w_up.reshape(depth * N_EXPERTS, d, d_exp)
    wd_all = exp_w_down.reshape(depth * N_EXPERTS, d_exp, d)

    z_np, rates_np, cos_np, sin_np = _hyena_tables(seq)
    cos_t = jnp.asarray(cos_np).astype(BF16)
    sin_t = jnp.asarray(sin_np).astype(BF16)
    w1p = jnp.zeros((depth, LANES, HY_FFN), F32).at[:, :HY_EMB].set(hy_w1)
    h_sum, h_diff = hyena_filters(jnp.asarray(z_np), jnp.asarray(rates_np), w1p, hy_b1.reshape(depth, 1, HY_FFN),
                                  hy_freq, hy_w2, hy_b2.reshape(depth, 1, HY_FFN), hy_w3)
    k_cos, k_sin = hyena_spectrum(cos_t, sin_t, h_sum, h_diff)

    n_steps = (2 * t + N_EXPERTS * (MOE_TILE - 1)) // MOE_TILE + 2

    tokens = jnp.concatenate([jnp.broadcast_to(meta.astype(x.dtype)[None], (bsz, N_META, d)), x], axis=1)
    h, hb = ln_rows(tokens.reshape(t, d), emb_ln_g, emb_ln_b)
    b_c = b_in[:, o_hy:]
    u_a = project(hb, w_a, 0, b_a[0].reshape(1, -1), F32)
    u_b = project(hb, w_b, 0, b_b[0].reshape(1, -1), BF16)
    u_c = project(hb, w_c, 0, b_c[0].reshape(1, -1), F32)
    for l in range(depth):
        u_a, u_b, u_c = (u.reshape(bsz, seq, -1) for u in (u_a, u_b, u_c))
        y_a = gla_branch(u_a, w2f[l], gla_b_f[l], w2b[l], gla_b_b[l], gla_norm_g[l])
        y_b = swa_branch(u_b, swa_sink[l], swa_norm_g[l])
        y_c = hyena_branch(u_c, hy_conv_w[l], hy_conv_b[l].reshape(1, -1), cos_t, sin_t, k_cos, k_sin, l,
                           hy_skip[l], hy_norm_g[l])
        h1, routed = out_projection(alpha, y_a.reshape(t, -1), y_b.reshape(t, -1), y_c.reshape(t, -1), h,
                                    w_out_b, l, b_out[l], ln1_g[l], ln1_b[l], w_router, b_router)
        expert_ids = routed[:, 0:2].astype(jnp.int32)
        row_token, row_dest, tile_expert, n_used, run_start, next_expert = dispatch_plan(expert_ids, n_steps)
        base = l * N_EXPERTS
        y_assign = expert_mlps(h1, wg_all, wu_all, wd_all, tile_expert + base, n_used, row_token, row_dest,
                               run_start, jnp.where(next_expert >= 0, next_expert + base, -1))
        if l + 1 < depth:
            h, u_a, u_b, u_c = combine_ln_project(
                alpha, y_assign, h1, routed, ln2_g[l], ln2_b[l], w_a, w_b, w_c, l + 1,
                b_a[l + 1].reshape(1, -1), b_b[l + 1].reshape(1, -1), b_c[l + 1].reshape(1, -1))
        else:
            h, _ = combine_ln(alpha, y_assign, h1, routed, ln2_g[l], ln2_b[l])
    return h.reshape(bsz, seq, d)[:, N_META:]
```

```python
import functools
import math

import jax
import jax.numpy as jnp
import numpy as np
from jax import lax
from jax.experimental import pallas as pl
from jax.experimental.pallas import tpu as pltpu

F32 = jnp.float32
BF16 = jnp.bfloat16
HIGHEST = lax.Precision.HIGHEST

D_MODEL = 2048
N_META = 16
GLA_HEADS = 4
GLA_DK = 64
GLA_DV = 128
GLA_WIDTH = GLA_HEADS * GLA_DV
GLA_RANK = 16
GLA_TAU = 16.0
GLA_CHUNK = 64
GLA_UNROLL = 11
SWA_HEADS = 8
SWA_KV_HEADS = 2
SWA_GROUP = SWA_HEADS // SWA_KV_HEADS
SWA_HEAD_DIM = 128
SWA_WIDTH = SWA_HEADS * SWA_HEAD_DIM
WINDOW = 128
HY_WIDTH = 512
HY_BANDS = 16
HY_EMB = 2 * HY_BANDS + 1
HY_FFN = 64
HY_FAST_PCT = 0.3
HY_SLOW_PCT = 1.5
HY_TARGET = 1e-2
N_GROUPS = 4
EXPERTS_PER_GROUP = 4
N_EXPERTS = N_GROUPS * EXPERTS_PER_GROUP
D_EXPERT = 1024
LN_EPS = 1e-5
RMS_EPS = 1e-6
NEG = -1e30

LANES = 128
A_COLS = 1664
GATE_COL0 = 1536
ROW_TILE = 384
FINAL_TILE = 256
FUSED_TILE = 192
MOE_TILE = 256
VMEM_LIMIT = 56 * 1024 * 1024


def _cparams(sem):
    return pltpu.CompilerParams(dimension_semantics=sem, vmem_limit_bytes=VMEM_LIMIT)


def _resident(shape, index_map):
    return pl.BlockSpec(shape, index_map, pipeline_mode=pl.Buffered(1))


def _dot(a, b, **kw):
    return jnp.dot(a, b, preferred_element_type=F32, **kw)


def _dot_nt(a, b):
    return lax.dot_general(a, b, (((1,), (1,)), ((), ())), preferred_element_type=F32)


def _ln(x, g, b):
    mu = jnp.mean(x, axis=-1, keepdims=True)
    xc = x - mu
    var = jnp.mean(xc * xc, axis=-1, keepdims=True)
    return xc * lax.rsqrt(var + LN_EPS) * g + b


def _ln_rows_kernel(x_ref, g_ref, b_ref, o_ref, ob_ref):
    y = _ln(x_ref[...], g_ref[...], b_ref[...])
    o_ref[...] = y
    ob_ref[...] = y.astype(BF16)


def ln_rows(x, g, b):
    t, d = x.shape
    row = pl.BlockSpec((ROW_TILE, d), lambda i: (i, 0))
    vec = pl.BlockSpec((1, d), lambda i: (0, 0))
    return pl.pallas_call(
        _ln_rows_kernel,
        grid=(t // ROW_TILE,),
        in_specs=[row, vec, vec],
        out_specs=[row, row],
        out_shape=[jax.ShapeDtypeStruct((t, d), F32), jax.ShapeDtypeStruct((t, d), BF16)],
        compiler_params=_cparams(("parallel",)),
        name="ln_rows",
    )(x, g.reshape(1, d), b.reshape(1, d))


IN_GATES, IN_R, IN_SWA, IN_HY = 1024, 1056, 1568, 3104


def _regroup_kernel(w_ref, wa_ref, wb_ref, wc_ref):
    w = w_ref[0]
    rows = w.shape[0]
    pad = jnp.zeros((rows, wa_ref.shape[2] - IN_SWA), F32)
    wa = jnp.concatenate([w[:, :IN_GATES], w[:, IN_R:IN_SWA], w[:, IN_GATES:IN_R], pad], axis=1)
    wa_ref[0] = wa.astype(BF16)
    q_end = IN_SWA + SWA_WIDTH
    wb_ref[0] = jnp.concatenate([w[:, IN_SWA:q_end] * SWA_Q_SCALE, w[:, q_end:IN_HY]], axis=1).astype(BF16)
    wc_ref[0] = w[:, IN_HY:].astype(BF16)


def regroup_in_weights(w_in):
    depth, d, cols = w_in.shape
    rows = 256
    nb, nc = IN_HY - IN_SWA, cols - IN_HY

    def out(n):
        return pl.BlockSpec((1, rows, n), lambda l, i: (l, i, 0))

    return pl.pallas_call(
        _regroup_kernel,
        grid=(depth, d // rows),
        in_specs=[pl.BlockSpec((1, rows, cols), lambda l, i: (l, i, 0))],
        out_specs=[out(A_COLS), out(nb), out(nc)],
        out_shape=[jax.ShapeDtypeStruct((depth, d, A_COLS), BF16), jax.ShapeDtypeStruct((depth, d, nb), BF16),
                   jax.ShapeDtypeStruct((depth, d, nc), BF16)],
        compiler_params=_cparams(("parallel", "parallel")),
        name="regroup_in_weights",
    )(w_in)

def _proj_kernel(x_ref, w_ref, b_ref, o_ref):
    o_ref[...] = (_dot(x_ref[...], w_ref[0]) + b_ref[...]).astype(o_ref.dtype)


def project(x, w, layer, b, out_dtype):
    t, k = x.shape
    n = w.shape[2]
    return pl.pallas_call(
        _proj_kernel,
        grid=(t // ROW_TILE,),
        in_specs=[pl.BlockSpec((ROW_TILE, k), lambda i: (i, 0)),
                  _resident((1, k, n), lambda i: (layer, 0, 0)),
                  pl.BlockSpec((1, n), lambda i: (0, 0))],
        out_specs=pl.BlockSpec((ROW_TILE, n), lambda i: (i, 0)),
        out_shape=jax.ShapeDtypeStruct((t, n), out_dtype),
        compiler_params=_cparams(("parallel",)),
        name="in_proj",
    )(x, w, b)


def _log_sigmoid(x):
    return jnp.minimum(x, 0.0) - jnp.log(1.0 + jnp.exp(-jnp.abs(x)))


def _chunk_scan(x, reverse):
    c = x.shape[0]
    idx = lax.broadcasted_iota(jnp.int32, x.shape, 0)
    s = 1
    while s < c:
        if reverse:
            x = x + jnp.where(idx < c - s, pltpu.roll(x, c - s, axis=0), 0.0)
        else:
            x = x + jnp.where(idx >= s, pltpu.roll(x, s, axis=0), 0.0)
        s *= 2
    return x


def _gla_kernel(q_ref, k_ref, v_ref, r_ref, gt_ref, w2f_ref, bf_ref, w2b_ref, bb_ref, ng_ref, o_ref,
                qs, ks, vs, lfs, lbs, acc, acc_b, st):
    seq = q_ref.shape[1]
    seq_pad = qs.shape[0]
    n_chunks = seq_pad // GLA_CHUNK
    tail = seq_pad - seq
    c = GLA_CHUNK

    gates = gt_ref[0]
    log_f = _log_sigmoid(_dot(gates, w2f_ref[...], precision=HIGHEST) + bf_ref[...]) * (1.0 / GLA_TAU)
    log_b = _log_sigmoid(_dot(gates, w2b_ref[...], precision=HIGHEST) + bb_ref[...]) * (1.0 / GLA_TAU)
    for dst, src in ((qs, q_ref[0]), (ks, k_ref[0]), (vs, v_ref[0]), (lfs, log_f), (lbs, log_b)):
        dst[0:seq, :] = src
        dst[seq:seq_pad, :] = jnp.zeros((tail, dst.shape[1]), F32)

    lane = lax.broadcasted_iota(jnp.int32, (c, 2 * GLA_DK), 1)
    row = lax.broadcasted_iota(jnp.int32, (c, c), 0)
    col = lax.broadcasted_iota(jnp.int32, (c, c), 1)
    scale = GLA_DK ** -0.5

    st[...] = jnp.zeros(st.shape, F32)

    def chunk(n, log_ref, reverse, out_ref):
        keep = (col > row) if reverse else (col <= row)
        r0 = pl.multiple_of(n * c, c)
        cum = _chunk_scan(log_ref[pl.ds(r0, c), :], reverse)
        tot = cum[0:1, :] if reverse else cum[c - 1:c, :]
        q = qs[pl.ds(r0, c), :]
        k = ks[pl.ds(r0, c), :]
        q_dec = q * jnp.exp(cum) * scale
        k_inv = (k * jnp.exp(-cum)).astype(BF16)
        k_dec = (k * jnp.exp(tot - cum)).astype(BF16)
        decay = jnp.exp(tot)
        for h in range(2):
            slot = 2 * int(reverse) + h
            head_lanes = (lane < GLA_DK) if h == 0 else (lane >= GLA_DK)
            qm = jnp.where(head_lanes, q_dec, 0.0).astype(BF16)
            att = jnp.where(keep, _dot_nt(qm, k_inv), 0.0)
            vh = vs[pl.ds(r0, c), h * GLA_DV:(h + 1) * GLA_DV]
            state = st[slot]
            out_ref[pl.ds(r0, c), h * GLA_DV:(h + 1) * GLA_DV] = (
                _dot(att.astype(BF16), vh.astype(BF16)) + _dot_nt(qm, state.astype(BF16)))
            st[slot] = state * decay + _dot(vh.T.astype(BF16), k_dec)

    def body(i, carry):
        chunk(i, lfs, False, acc)
        chunk(n_chunks - 1 - i, lbs, True, acc_b)
        return carry

    lax.fori_loop(0, n_chunks, body, 0, unroll=GLA_UNROLL)

    r = r_ref[0]
    gate = r * (1.0 / (1.0 + jnp.exp(-r)))
    for h in range(2):
        sl = slice(h * GLA_DV, (h + 1) * GLA_DV)
        o = acc[0:seq, sl] + acc_b[0:seq, sl]
        y = o * lax.rsqrt(jnp.mean(o * o, axis=-1, keepdims=True) + RMS_EPS) * ng_ref[:, sl]
        o_ref[0, :, sl] = (y * gate[:, sl]).astype(o_ref.dtype)


def gla_branch(u_a, w2f, b_f, w2b, b_b, norm_g):
    bsz, seq, _ = u_a.shape
    seq_pad = -(-seq // GLA_CHUNK) * GLA_CHUNK
    pair = 2 * GLA_DK

    def col(width, first_block):
        return pl.BlockSpec((1, seq, width), lambda b, p: (b, 0, first_block + p))

    vec128 = pl.BlockSpec((1, pair), lambda b, p: (0, p))
    return pl.pallas_call(
        _gla_kernel,
        grid=(bsz, GLA_HEADS // 2),
        in_specs=[col(pair, 0), col(pair, 2), col(2 * GLA_DV, 2), col(2 * GLA_DV, 4),
                  pl.BlockSpec((1, seq, LANES), lambda b, p: (b, 0, GATE_COL0 // LANES)),
                  pl.BlockSpec((LANES, pair), lambda b, p: (0, p)), vec128,
                  pl.BlockSpec((LANES, pair), lambda b, p: (0, p)), vec128,
                  pl.BlockSpec((1, 2 * GLA_DV), lambda b, p: (0, p))],
        out_specs=pl.BlockSpec((1, seq, 2 * GLA_DV), lambda b, p: (b, 0, p)),
        out_shape=jax.ShapeDtypeStruct((bsz, seq, GLA_WIDTH), BF16),
        scratch_shapes=[pltpu.VMEM((seq_pad, pair), F32), pltpu.VMEM((seq_pad, pair), F32),
                        pltpu.VMEM((seq_pad, 2 * GLA_DV), F32), pltpu.VMEM((seq_pad, pair), F32),
                        pltpu.VMEM((seq_pad, pair), F32), pltpu.VMEM((seq_pad, 2 * GLA_DV), F32),
                        pltpu.VMEM((seq_pad, 2 * GLA_DV), F32), pltpu.VMEM((4, GLA_DV, pair), F32)],
        compiler_params=_cparams(("parallel", "parallel")),
        name="gla",
    )(u_a, u_a, u_a, u_a, u_a, w2f, b_f.reshape(1, -1), w2b, b_b.reshape(1, -1), norm_g.reshape(1, -1))


LOG2E = 1.4426950408889634
SWA_Q_SCALE = SWA_HEAD_DIM ** -0.5 * LOG2E


def _swa_slope2(h):
    return LOG2E * 2.0 ** (-8.0 * (h + 1) / SWA_HEADS)


def _swa_kernel(sink_ref, q_ref, k_ref, v_ref, ng_ref, o_ref, band_bias, meta_bias):
    seq = q_ref.shape[1]
    blk = WINDOW
    n_blocks = (seq - N_META) // blk
    hd = SWA_HEAD_DIM
    grp = SWA_GROUP

    def stacked(fn, nq):
        row = lax.broadcasted_iota(jnp.int32, (grp * nq, 1), 0)
        col = jnp.full((grp * nq, 1), fn(grp - 1), F32)
        for g in range(grp - 2, -1, -1):
            col = jnp.where(row < (g + 1) * nq, fn(g), col)
        return col

    def band_bias_tile(kv, nq, nk, delta):
        qi = lax.broadcasted_iota(jnp.int32, (grp * nq, nk), 0) & (nq - 1)
        ki = lax.broadcasted_iota(jnp.int32, (grp * nq, nk), 1)
        dist = jnp.abs(qi - ki + delta)
        slope = stacked(lambda g: _swa_slope2(kv * grp + g), nq)
        return jnp.where(dist <= WINDOW, -slope * dist.astype(F32), NEG)

    def meta_bias_tile(kv, nq, among_meta):
        qi = lax.broadcasted_iota(jnp.int32, (grp * nq, blk), 0) & (nq - 1)
        ki = lax.broadcasted_iota(jnp.int32, (grp * nq, blk), 1)
        rel = jnp.abs(qi - ki) if among_meta else qi - ki
        slope = stacked(lambda g: _swa_slope2(kv * grp + g), nq)
        return jnp.where(ki < N_META, -slope * rel.astype(F32), NEG)

    def attend(r0, nq, w0, nk, band_fn, meta_fn):
        outs = [None] * SWA_HEADS
        for kv in range(SWA_KV_HEADS):
            ks = slice(kv * hd, (kv + 1) * hd)
            q4 = jnp.concatenate([q_ref[0, pl.ds(r0, nq), (kv * grp + g) * hd:(kv * grp + g + 1) * hd]
                                  for g in range(grp)], axis=0)
            s_band = _dot_nt(q4, k_ref[0, pl.ds(w0, nk), ks]) + band_fn(kv)
            s_meta = _dot_nt(q4, k_ref[0, 0:blk, ks]) + meta_fn(kv)
            sink = stacked(lambda g: sink_ref[kv * grp + g] * LOG2E, nq)
            m = jnp.maximum(jnp.maximum(jnp.max(s_band, axis=-1, keepdims=True),
                                        jnp.max(s_meta, axis=-1, keepdims=True)), sink)
            p_band = jnp.exp2(s_band - m)
            p_meta = jnp.exp2(s_meta - m)
            den = (jnp.sum(p_band, axis=-1, keepdims=True) + jnp.sum(p_meta, axis=-1, keepdims=True)
                   + jnp.exp2(sink - m))
            o4 = (_dot(p_band.astype(BF16), v_ref[0, pl.ds(w0, nk), ks])
                  + _dot(p_meta.astype(BF16), v_ref[0, 0:blk, ks])) * (1.0 / den)
            for g in range(grp):
                outs[kv * grp + g] = o4[g * nq:(g + 1) * nq]
        ssq = outs[0] * outs[0]
        for o in outs[1:]:
            ssq = ssq + o * o
        inv = lax.rsqrt(jnp.sum(ssq, axis=-1, keepdims=True) * (1.0 / SWA_WIDTH) + RMS_EPS)
        for h in range(SWA_HEADS):
            cols = slice(h * hd, (h + 1) * hd)
            o_ref[0, pl.ds(r0, nq), cols] = (outs[h] * inv * ng_ref[:, cols]).astype(o_ref.dtype)

    for kv in range(SWA_KV_HEADS):
        band_bias[kv] = band_bias_tile(kv, blk, 3 * blk, blk)
        meta_bias[kv] = meta_bias_tile(kv, blk, False)

    def meta_fn_at(r0):
        def fn(kv):
            slope = stacked(lambda g: _swa_slope2(kv * grp + g), blk)
            return meta_bias[kv] - slope * r0
        return fn

    last_r0 = N_META + (n_blocks - 1) * blk
    attend(0, N_META, N_META, blk, lambda kv: band_bias_tile(kv, N_META, blk, -N_META),
           lambda kv: meta_bias_tile(kv, N_META, True))
    attend(N_META, blk, N_META, 3 * blk, lambda kv: band_bias_tile(kv, blk, 3 * blk, 0), meta_fn_at(float(N_META)))
    attend(last_r0, blk, last_r0 - 2 * blk, 3 * blk, lambda kv: band_bias_tile(kv, blk, 3 * blk, 2 * blk),
           meta_fn_at(float(last_r0)))

    def body(j, carry):
        r0 = pl.multiple_of(N_META + j * blk, N_META)
        attend(r0, blk, pl.multiple_of(r0 - blk, N_META), 3 * blk, lambda kv: band_bias[kv],
               meta_fn_at(r0.astype(F32)))
        return carry

    lax.fori_loop(1, n_blocks - 1, body, 0)


def swa_branch(u_b, sink, norm_g):
    bsz, seq, _ = u_b.shape
    kvw = SWA_KV_HEADS * SWA_HEAD_DIM
    rows = SWA_GROUP * WINDOW
    return pl.pallas_call(
        _swa_kernel,
        grid=(bsz,),
        in_specs=[pl.BlockSpec(memory_space=pltpu.SMEM),
                  pl.BlockSpec((1, seq, SWA_WIDTH), lambda b: (b, 0, 0)),
                  pl.BlockSpec((1, seq, kvw), lambda b: (b, 0, SWA_WIDTH // kvw)),
                  pl.BlockSpec((1, seq, kvw), lambda b: (b, 0, SWA_WIDTH // kvw + 1)),
                  pl.BlockSpec((1, SWA_WIDTH), lambda b: (0, 0))],
        out_specs=pl.BlockSpec((1, seq, SWA_WIDTH), lambda b: (b, 0, 0)),
        out_shape=jax.ShapeDtypeStruct((bsz, seq, SWA_WIDTH), BF16),
        scratch_shapes=[pltpu.VMEM((SWA_KV_HEADS, rows, 3 * WINDOW), F32),
                        pltpu.VMEM((SWA_KV_HEADS, rows, WINDOW), F32)],
        compiler_params=_cparams(("parallel",)),
        name="swa",
    )(sink, u_b, u_b, u_b, norm_g.reshape(1, -1))


@functools.lru_cache(maxsize=None)
def _hyena_tables(seq):
    t = np.linspace(0.0, 1.0, seq, dtype=np.float32)[:, None]
    w = (np.float32(2.0 * math.pi) * np.arange(seq, dtype=np.float32)[:, None] / np.float32(seq)).astype(np.float32)
    bands = np.linspace(1e-4, HY_BANDS - 1, HY_BANDS, dtype=np.float32)
    arg = (bands * w).astype(np.float32).astype(np.float64)
    z = np.concatenate([t.astype(np.float64), np.cos(arg), -np.sin(arg)], axis=-1)
    z_pad = np.zeros((seq, LANES), np.float32)
    z_pad[:, :HY_EMB] = z.astype(np.float32)
    max_decay = math.log(HY_TARGET) / HY_FAST_PCT
    min_decay = math.log(HY_TARGET) / HY_SLOW_PCT
    rates = np.abs(np.linspace(min_decay, max_decay, HY_WIDTH, dtype=np.float32))[None, :]
    n = 2 * seq - 1
    fj = (np.arange(seq, dtype=np.int64)[:, None] * np.arange(seq, dtype=np.int64)[None, :]) % n
    ang = fj.astype(np.float64) * (2.0 * math.pi / n)
    return z_pad, rates.astype(np.float32), np.cos(ang).astype(np.float32), np.sin(ang).astype(np.float32)


def _hy_filter_kernel(z_ref, rate_ref, w1_ref, b1_ref, fr_ref, w2_ref, b2_ref, w3_ref, hs_ref, hd_ref):
    z = z_ref[...]
    h = jnp.sin(fr_ref[0, 0:1, :] * (_dot(z, w1_ref[0], precision=HIGHEST) + b1_ref[0]))
    h = jnp.sin(fr_ref[0, 1:2, :] * (_dot(h, w2_ref[0], precision=HIGHEST) + b2_ref[0]))
    h = _dot(h, w3_ref[0], precision=HIGHEST)
    window = jnp.exp(-z[:, 0:1] * rate_ref[...])
    h_fwd = h[:, :HY_WIDTH] * window
    h_bwd = h[:, HY_WIDTH:] * window
    first = lax.broadcasted_iota(jnp.int32, h_bwd.shape, 0) == 0
    h_bwd = jnp.where(first, 0.0, h_bwd)
    hs_ref[0] = h_fwd + h_bwd
    hd_ref[0] = h_bwd - h_fwd


def hyena_filters(z, rates, w1p, b1, freq, w2, b2, w3):
    depth = w1p.shape[0]
    seq = z.shape[0]

    def per_layer(*shape):
        return pl.BlockSpec((1,) + shape, lambda l: (l,) + (0,) * len(shape))

    out = jax.ShapeDtypeStruct((depth, seq, HY_WIDTH), F32)
    return pl.pallas_call(
        _hy_filter_kernel,
        grid=(depth,),
        in_specs=[pl.BlockSpec((seq, LANES), lambda l: (0, 0)), pl.BlockSpec((1, HY_WIDTH), lambda l: (0, 0)),
                  per_layer(LANES, HY_FFN), per_layer(1, HY_FFN), per_layer(2, HY_FFN),
                  per_layer(HY_FFN, HY_FFN), per_layer(1, HY_FFN), per_layer(HY_FFN, 2 * HY_WIDTH)],
        out_specs=[per_layer(seq, HY_WIDTH), per_layer(seq, HY_WIDTH)],
        out_shape=[out, out],
        compiler_params=_cparams(("parallel",)),
        name="hyena_filters",
    )(z, rates, w1p, b1, freq, w2, b2, w3)


def _split_bf16(x):
    hi = x.astype(BF16)
    return hi, (x - hi.astype(F32)).astype(BF16)


def _hy_spectrum_kernel(cos_ref, sin_ref, hs_ref, hd_ref, kc_ref, ks_ref):
    seq = cos_ref.shape[0]
    n = 2 * seq - 1
    f = lax.broadcasted_iota(jnp.int32, (seq, 1), 0)
    weight = jnp.where(f == 0, 1.0 / n, 2.0 / n)
    s_hi, s_lo = _split_bf16(hs_ref[0])
    d_hi, d_lo = _split_bf16(hd_ref[0])
    kc_ref[0] = (_dot(cos_ref[...], s_hi) + _dot(cos_ref[...], s_lo)) * weight
    ks_ref[0] = (_dot(sin_ref[...], d_hi) + _dot(sin_ref[...], d_lo)) * weight


def hyena_spectrum(cos_t, sin_t, h_sum, h_diff):
    depth, seq, _ = h_sum.shape
    half = HY_WIDTH // 2
    table = _resident((seq, seq), lambda l, c: (0, 0))
    blk = pl.BlockSpec((1, seq, half), lambda l, c: (l, 0, c))
    out = jax.ShapeDtypeStruct((depth, seq, HY_WIDTH), F32)
    return pl.pallas_call(
        _hy_spectrum_kernel,
        grid=(depth, 2),
        in_specs=[table, table, blk, blk],
        out_specs=[blk, blk],
        out_shape=[out, out],
        compiler_params=_cparams(("parallel", "parallel")),
        name="hyena_spectrum",
    )(cos_t, sin_t, h_sum, h_diff)


def _short_conv(u, w, b):
    seq = u.shape[0]
    row = lax.broadcasted_iota(jnp.int32, u.shape, 0)
    prev = jnp.where(row == 0, 0.0, pltpu.roll(u, 1, axis=0))
    nxt = jnp.where(row == seq - 1, 0.0, pltpu.roll(u, seq - 1, axis=0))
    return prev * w[0:1, :] + u * w[1:2, :] + nxt * w[2:3, :] + b


def _hyena_kernel(x0_ref, x1_ref, v_ref, w0_ref, w1_ref, wv_ref, b0_ref, b1_ref, bv_ref,
                  cos_ref, sin_ref, kc_ref, ks_ref, skip_ref, ng_ref, o_ref, acc):
    c = pl.program_id(1)
    half = x0_ref.shape[2]
    x0 = _short_conv(x0_ref[0], w0_ref[...], b0_ref[...])
    x1 = _short_conv(x1_ref[0], w1_ref[...], b1_ref[...])
    v = _short_conv(v_ref[0], wv_ref[...], bv_ref[...])
    g = x1 * v
    gb = g.astype(BF16)
    uc = _dot(cos_ref[...], gb)
    us = _dot(sin_ref[...], gb)
    kc = kc_ref[0]
    ks = ks_ref[0]
    pr = (uc * kc + us * ks).astype(BF16)
    pi = (uc * ks - us * kc).astype(BF16)
    y = _dot(cos_ref[...], pr) - _dot(sin_ref[...], pi) + g * skip_ref[...]
    y = x0 * y

    @pl.when(c == 0)
    def _():
        acc[:, 0:half] = y

    @pl.when(c == 1)
    def _():
        acc[:, half:2 * half] = y
        full = acc[...]
        o_ref[0] = (full * lax.rsqrt(jnp.mean(full * full, axis=-1, keepdims=True) + RMS_EPS)
                    * ng_ref[...]).astype(o_ref.dtype)


def hyena_branch(u_c, conv_w, conv_b, cos_t, sin_t, k_cos, k_sin, layer, skip, norm_g):
    bsz, seq, _ = u_c.shape
    half = HY_WIDTH // 2

    def stream(first_block):
        return pl.BlockSpec((1, seq, half), lambda b, c: (b, 0, first_block + c))

    def wcol(rows, first_block):
        return pl.BlockSpec((rows, half), lambda b, c: (0, first_block + c))

    table = _resident((seq, seq), lambda b, c: (0, 0))
    spec = pl.BlockSpec((1, seq, half), lambda b, c: (layer, 0, c))
    return pl.pallas_call(
        _hyena_kernel,
        grid=(bsz, 2),
        in_specs=[stream(0), stream(2), stream(4), wcol(3, 0), wcol(3, 2), wcol(3, 4),
                  wcol(1, 0), wcol(1, 2), wcol(1, 4), table, table, spec, spec, wcol(1, 0),
                  pl.BlockSpec((1, HY_WIDTH), lambda b, c: (0, 0))],
        out_specs=pl.BlockSpec((1, seq, HY_WIDTH), lambda b, c: (b, 0, 0)),
        out_shape=jax.ShapeDtypeStruct((bsz, seq, HY_WIDTH), BF16),
        scratch_shapes=[pltpu.VMEM((seq, HY_WIDTH), F32)],
        compiler_params=_cparams(("parallel", "arbitrary")),
        name="hyena",
    )(u_c, u_c, u_c, conv_w, conv_w, conv_w, conv_b, conv_b, conv_b, cos_t, sin_t, k_cos, k_sin,
      skip.reshape(1, -1), norm_g.reshape(1, -1))


def _first_index_of_max(x, valid, lane):
    m = jnp.max(jnp.where(valid, x, NEG), axis=-1, keepdims=True)
    idx = jnp.min(jnp.where(valid & (x == m), lane, float(LANES)), axis=-1, keepdims=True)
    return m, idx


def _route(x):
    lane = lax.broadcasted_iota(jnp.int32, x.shape, 1).astype(F32)
    is_group = lane < N_GROUPS
    gm, g_idx = _first_index_of_max(x, is_group, lane)
    g_top = 1.0 / jnp.sum(jnp.where(is_group, jnp.exp(x - gm), 0.0), axis=-1, keepdims=True)
    lo = N_GROUPS + g_idx * EXPERTS_PER_GROUP
    in_group = (lane >= lo) & (lane < lo + EXPERTS_PER_GROUP)
    m1, i1 = _first_index_of_max(x, in_group, lane)
    m2, i2 = _first_index_of_max(x, in_group & (lane != i1), lane)
    e2 = jnp.exp(m2 - m1)
    w1 = g_top / (1.0 + e2)
    w2 = g_top * e2 / (1.0 + e2)
    return jnp.where(lane == 0, i1 - N_GROUPS,
                     jnp.where(lane == 1, i2 - N_GROUPS,
                               jnp.where(lane == 2, w1, jnp.where(lane == 3, w2, 0.0))))


def _outproj_kernel(alpha, ya_ref, yb_ref, yc_ref, h_ref, w_ref, b_ref, g_ref, be_ref, wr_ref, br_ref,
                    h1_ref, rt_ref, z_scr):
    i = pl.program_id(0)
    last = pl.num_programs(0) - 1
    na = ya_ref.shape[1]
    nb = yb_ref.shape[1]

    def project(slot):
        mix = (_dot(ya_ref[...], w_ref[0, 0:na, :]) + _dot(yb_ref[...], w_ref[0, na:na + nb, :])
               + _dot(yc_ref[...], w_ref[0, na + nb:, :]) + b_ref[...])
        z_scr[slot] = alpha * h_ref[...] + mix

    def finish(slot):
        h1 = _ln(z_scr[slot], g_ref[...], be_ref[...])
        h1_ref[...] = h1
        rt_ref[...] = _route(_dot(h1.astype(BF16), wr_ref[0]) + br_ref[0])

    @pl.when(i == 0)
    def _():
        project(0)

    for cur in range(2):
        @pl.when((i % 2 == cur) & (i > 0) & (i < last))
        def _():
            finish(1 - cur)
            project(cur)

        @pl.when((i % 2 == cur) & (i == last))
        def _():
            finish(1 - cur)


def out_projection(alpha, y_a, y_b, y_c, h, w, layer, b, g, be, w_router, b_router):
    t, d = h.shape
    n_tiles = t // ROW_TILE

    def row_in(width):
        return pl.BlockSpec((ROW_TILE, width), lambda i: (jnp.minimum(i, n_tiles - 1), 0))

    def row_out(width):
        return pl.BlockSpec((ROW_TILE, width), lambda i: (jnp.maximum(i - 1, 0), 0))

    def vec():
        return pl.BlockSpec((1, d), lambda i: (0, 0))

    return pl.pallas_call(
        functools.partial(_outproj_kernel, alpha),
        grid=(n_tiles + 1,),
        in_specs=[row_in(y_a.shape[1]), row_in(y_b.shape[1]), row_in(y_c.shape[1]), row_in(d),
                  _resident((1, d, d), lambda i: (layer, 0, 0)), vec(), vec(), vec(),
                  _resident((1, d, LANES), lambda i: (layer, 0, 0)),
                  pl.BlockSpec((1, 1, LANES), lambda i: (layer, 0, 0))],
        out_specs=[row_out(d), row_out(LANES)],
        out_shape=[jax.ShapeDtypeStruct((t, d), F32), jax.ShapeDtypeStruct((t, LANES), F32)],
        scratch_shapes=[pltpu.VMEM((2, ROW_TILE, d), F32)],
        compiler_params=_cparams(("arbitrary",)),
        name="out_proj_ln",
    )(y_a, y_b, y_c, h, w, b.reshape(1, d), g.reshape(1, d), be.reshape(1, d), w_router, b_router)


def dispatch_plan(expert_ids, n_steps):
    t = expert_ids.shape[0]
    tm = MOE_TILE
    flat = expert_ids.reshape(-1)
    experts = jnp.arange(N_EXPERTS, dtype=jnp.int32)
    counts = jnp.sum((flat[:, None] == experts[None, :]).astype(jnp.int32), axis=0)
    padded = (counts + tm - 1) // tm * tm
    ends = jnp.cumsum(padded)
    pad_needed = jnp.arange(tm, dtype=jnp.int32)[None, :] < (padded - counts)[:, None]
    pad_keys = jnp.where(pad_needed, 2 * experts[:, None] + 1, 2 * N_EXPERTS).reshape(-1)
    keys = jnp.concatenate([2 * flat, pad_keys])
    vals = jnp.concatenate([jnp.arange(2 * t, dtype=jnp.int32), jnp.full((N_EXPERTS * tm,), 2 * t, jnp.int32)])
    _, ordered = lax.sort((keys, vals), num_keys=1)
    ordered = jnp.pad(ordered, (tm, (n_steps + 1) * tm - tm - ordered.shape[0]), constant_values=2 * t)
    lane = jnp.arange(ordered.shape[0], dtype=jnp.int32) % tm
    row_assign = jnp.where(ordered >= 2 * t, 2 * t + lane, ordered)
    plane = t + MOE_TILE // 2
    row_token = jnp.minimum(row_assign >> 1, t - 1)
    row_dest = (row_assign >> 1) + (row_assign & 1) * plane
    n_used = (ends[-1] // MOE_TILE).astype(jnp.int32)
    step = jnp.arange(n_steps, dtype=jnp.int32)
    tile_start = jnp.minimum(step, n_used - 1) * MOE_TILE
    tile_expert = jnp.sum((tile_start[:, None] >= ends[None, :]).astype(jnp.int32), axis=1)
    run_start = (step < n_used) & ((step == 0) | (tile_expert != jnp.roll(tile_expert, 1)))
    after = ends[tile_expert] // MOE_TILE
    next_expert = jnp.where(after < n_used, tile_expert[jnp.minimum(after, n_steps - 1)], -1)
    return (row_token, row_dest, tile_expert.astype(jnp.int32), n_used.reshape(1), run_start.astype(jnp.int32),
            next_expert.astype(jnp.int32))


def _moe_kernel(te_ref, nu_ref, rt_ref, rd_ref, rs_ref, ne_ref, x_hbm, wg_hbm, wu_hbm, wd_hbm, y_hbm,
                xbuf, ybuf, land_g, land_u, land_d, wg_ref, wu_ref, wd_ref, gsem, ssem, wsem):
    i = pl.program_id(0)
    n_used = nu_ref[0]
    tm = xbuf.shape[1]

    def weight_copies(e):
        return (pltpu.make_async_copy(wg_hbm.at[e], land_g, wsem.at[0]),
                pltpu.make_async_copy(wu_hbm.at[e], land_u, wsem.at[1]),
                pltpu.make_async_copy(wd_hbm.at[e], land_d, wsem.at[2]))

    @pl.when(i == 0)
    def _():
        for cp in weight_copies(te_ref[0]):
            cp.start()

    @pl.when(rs_ref[i] == 1)
    def _():
        for cp in weight_copies(te_ref[i]):
            cp.wait()
        wg_ref[...] = land_g[...].astype(BF16)
        wu_ref[...] = land_u[...].astype(BF16)
        wd_ref[...] = land_d[...].astype(BF16)

        @pl.when(ne_ref[i] >= 0)
        def _():
            for cp in weight_copies(ne_ref[i]):
                cp.start()

    def gather_copy(tile, slot, r):
        tok = rt_ref[(tile + 1) * tm + r]
        return pltpu.make_async_copy(x_hbm.at[pl.ds(tok, 1)], xbuf.at[slot, pl.ds(r, 1)], gsem.at[slot])

    def scatter_copy(tile, slot, r):
        dst = rd_ref[(tile + 1) * tm + r]
        return pltpu.make_async_copy(ybuf.at[slot, pl.ds(r, 1)], y_hbm.at[pl.ds(dst, 1)], ssem.at[slot])

    def wait_gather(slot):
        pltpu.make_async_copy(x_hbm.at[pl.ds(0, tm)], xbuf.at[slot], gsem.at[slot]).wait()

    def wait_scatter(slot):
        pltpu.make_async_copy(ybuf.at[slot], y_hbm.at[pl.ds(0, tm)], ssem.at[slot]).wait()

    def looped(copy_fn, tile, slot):
        def body(r, carry):
            copy_fn(tile, slot, r).start()
            return carry
        lax.fori_loop(0, tm, body, 0)

    @pl.when(i == 0)
    def _():
        ybuf[...] = jnp.zeros(ybuf.shape, ybuf.dtype)
        looped(gather_copy, 0, 0)

    for cur in range(2):
        nxt = 1 - cur

        @pl.when((i % 2 == cur) & (i <= n_used))
        def _():
            wait_gather(cur)

        @pl.when((i % 2 == cur) & (i >= 1) & (i <= n_used))
        def _():
            wait_scatter(cur)

        @pl.when((i % 2 == cur) & (i < n_used))
        def _():
            x = xbuf[cur].astype(BF16)

            def neighbour_copies(lo, hi):
                for r in range(lo, hi):
                    gather_copy(i + 1, nxt, r).start()
                    scatter_copy(i - 1, nxt, r).start()

            quarter = tm // 4
            neighbour_copies(0, quarter)
            a = _dot(x, wg_ref[...])
            neighbour_copies(quarter, 2 * quarter)
            u = _dot(x, wu_ref[...])
            neighbour_copies(2 * quarter, 3 * quarter)
            hmid = (a * (1.0 / (1.0 + jnp.exp(-a))) * u).astype(BF16)
            neighbour_copies(3 * quarter, tm)
            ybuf[cur] = _dot(hmid, wd_ref[...])

        @pl.when((i % 2 == cur) & (i == n_used))
        def _():
            looped(scatter_copy, i - 1, nxt)
            wait_scatter(nxt)


def expert_mlps(x, w_gate, w_up, w_down, tile_expert, n_used, row_token, row_dest, run_start, next_expert):
    t, d = x.shape
    n_steps = tile_expert.shape[0]
    f = w_gate.shape[2]
    hbm = pl.BlockSpec(memory_space=pl.ANY)
    grid_spec = pltpu.PrefetchScalarGridSpec(
        num_scalar_prefetch=6,
        grid=(n_steps,),
        in_specs=[hbm, hbm, hbm, hbm],
        out_specs=hbm,
        scratch_shapes=[pltpu.VMEM((2, MOE_TILE, d), F32), pltpu.VMEM((2, MOE_TILE, d), F32),
                        pltpu.VMEM((d, f), F32), pltpu.VMEM((d, f), F32), pltpu.VMEM((f, d), F32),
                        pltpu.VMEM((d, f), BF16), pltpu.VMEM((d, f), BF16), pltpu.VMEM((f, d), BF16),
                        pltpu.SemaphoreType.DMA((2,)), pltpu.SemaphoreType.DMA((2,)), pltpu.SemaphoreType.DMA((3,))],
    )
    return pl.pallas_call(
        _moe_kernel,
        grid_spec=grid_spec,
        out_shape=jax.ShapeDtypeStruct((2 * (t + MOE_TILE // 2), d), F32),
        compiler_params=_cparams(("arbitrary",)),
        name="moe_experts",
    )(tile_expert, n_used, row_token, row_dest, run_start, next_expert, x, w_gate, w_up, w_down
      ).reshape(2, t + MOE_TILE // 2, d)


def _combine_kernel(alpha, y1_ref, y2_ref, h_ref, rw_ref, g_ref, b_ref, o_ref, ob_ref):
    rw = rw_ref[...]
    moe = y1_ref[0] * rw[:, 2:3] + y2_ref[0] * rw[:, 3:4]
    y = _ln(alpha * h_ref[...] + moe, g_ref[...], b_ref[...])
    o_ref[...] = y
    ob_ref[...] = y.astype(BF16)


def combine_ln(alpha, y_assign, h, route_out, g, b):
    t, d = h.shape
    tm = ROW_TILE
    row = pl.BlockSpec((tm, d), lambda i: (i, 0))
    vec = pl.BlockSpec((1, d), lambda i: (0, 0))
    return pl.pallas_call(
        functools.partial(_combine_kernel, alpha),
        grid=(t // tm,),
        in_specs=[pl.BlockSpec((1, tm, d), lambda i: (0, i, 0)), pl.BlockSpec((1, tm, d), lambda i: (1, i, 0)), row,
                  pl.BlockSpec((tm, LANES), lambda i: (i, 0)), vec, vec],
        out_specs=[row, row],
        out_shape=[jax.ShapeDtypeStruct((t, d), F32), jax.ShapeDtypeStruct((t, d), BF16)],
        compiler_params=_cparams(("parallel",)),
        name="moe_combine_ln",
    )(y_assign, y_assign, h, route_out, g.reshape(1, d), b.reshape(1, d))


def _combine_final_kernel(alpha, y1_ref, y2_ref, h_ref, rw_ref, g_ref, b_ref, o_ref):
    rw = rw_ref[...]
    moe = y1_ref[...] * rw[:, 2:3] + y2_ref[...] * rw[:, 3:4]
    o_ref[0] = _ln(alpha * h_ref[...] + moe, g_ref[...], b_ref[...])


def combine_ln_final(alpha, y_assign, h, route_out, g, b, bsz, seq):
    t, d = h.shape
    tm = FINAL_TILE
    n_j = (seq - N_META) // tm

    plane = y_assign.shape[1]

    def first_row(bi, j, k=0):
        return pl.multiple_of(k * plane + bi * seq + N_META + j * tm, 8)

    rows, cols, lanes = pl.Element(tm), pl.Element(d), pl.Element(LANES)
    vec = pl.BlockSpec((1, d), lambda bi, j: (0, 0))
    y_assign = y_assign.reshape(2 * plane, d)
    return pl.pallas_call(
        functools.partial(_combine_final_kernel, alpha),
        grid=(bsz, n_j),
        in_specs=[pl.BlockSpec((rows, cols), lambda bi, j: (first_row(bi, j, 0), 0)),
                  pl.BlockSpec((rows, cols), lambda bi, j: (first_row(bi, j, 1), 0)),
                  pl.BlockSpec((rows, cols), lambda bi, j: (first_row(bi, j), 0)),
                  pl.BlockSpec((rows, lanes), lambda bi, j: (first_row(bi, j), 0)), vec, vec],
        out_specs=pl.BlockSpec((1, tm, d), lambda bi, j: (bi, j, 0)),
        out_shape=jax.ShapeDtypeStruct((bsz, seq - N_META, d), F32),
        compiler_params=_cparams(("parallel", "parallel")),
        name="moe_combine_ln_final",
    )(y_assign, y_assign, h, route_out, g.reshape(1, d), b.reshape(1, d))


def _combine_project_kernel(alpha, y1_ref, y2_ref, h_ref, rw_ref, g_ref, b_ref, wa_ref, wb_ref, wc_ref,
                            ba_ref, bb_ref, bc_ref, o_ref, ua_ref, ub_ref, uc_ref, hb_scr):
    i = pl.program_id(0)
    last = pl.num_programs(0) - 1

    def combine(slot):
        rw = rw_ref[...]
        moe = y1_ref[0] * rw[:, 2:3] + y2_ref[0] * rw[:, 3:4]
        y = _ln(alpha * h_ref[...] + moe, g_ref[...], b_ref[...])
        o_ref[...] = y
        hb_scr[slot] = y.astype(BF16)

    def project(slot):
        x = hb_scr[slot]
        ua_ref[...] = _dot(x, wa_ref[0]) + ba_ref[...]
        ub_ref[...] = (_dot(x, wb_ref[0]) + bb_ref[...]).astype(ub_ref.dtype)
        uc_ref[...] = _dot(x, wc_ref[0]) + bc_ref[...]

    @pl.when(i == 0)
    def _():
        combine(0)

    for cur in range(2):
        @pl.when((i % 2 == cur) & (i > 0) & (i < last))
        def _():
            combine(cur)
            project(1 - cur)

        @pl.when((i % 2 == cur) & (i == last))
        def _():
            project(1 - cur)


def combine_ln_project(alpha, y_assign, h, route_out, g, b, w_a, w_b, w_c, layer, b_a, b_b, b_c):
    t, d = h.shape
    tm = FUSED_TILE
    n_tiles = t // tm

    def cur(i):
        return jnp.minimum(i, n_tiles - 1)

    def prev(i):
        return jnp.maximum(i - 1, 0)

    def weight(w):
        return _resident((1, d, w.shape[2]), lambda i: (layer, 0, 0))

    def bias(n):
        return pl.BlockSpec((1, n), lambda i: (0, 0))

    na, nb, nc = w_a.shape[2], w_b.shape[2], w_c.shape[2]
    vec = pl.BlockSpec((1, d), lambda i: (0, 0))
    return pl.pallas_call(
        functools.partial(_combine_project_kernel, alpha),
        grid=(n_tiles + 1,),
        in_specs=[pl.BlockSpec((1, tm, d), lambda i: (0, cur(i), 0)),
                  pl.BlockSpec((1, tm, d), lambda i: (1, cur(i), 0)),
                  pl.BlockSpec((tm, d), lambda i: (cur(i), 0)),
                  pl.BlockSpec((tm, LANES), lambda i: (cur(i), 0)), vec, vec,
                  weight(w_a), weight(w_b), weight(w_c), bias(na), bias(nb), bias(nc)],
        out_specs=[pl.BlockSpec((tm, d), lambda i: (cur(i), 0)),
                   pl.BlockSpec((tm, na), lambda i: (prev(i), 0)),
                   pl.BlockSpec((tm, nb), lambda i: (prev(i), 0)),
                   pl.BlockSpec((tm, nc), lambda i: (prev(i), 0))],
        out_shape=[jax.ShapeDtypeStruct((t, d), F32), jax.ShapeDtypeStruct((t, na), F32),
                   jax.ShapeDtypeStruct((t, nb), BF16), jax.ShapeDtypeStruct((t, nc), F32)],
        scratch_shapes=[pltpu.VMEM((2, tm, d), BF16)],
        compiler_params=_cparams(("arbitrary",)),
        name="combine_ln_in_proj",
    )(y_assign, y_assign, h, route_out, g.reshape(1, d), b.reshape(1, d), w_a, w_b, w_c, b_a, b_b, b_c)


def kernel(x, meta, emb_ln_g, emb_ln_b, w_in, b_in, gla_w2_f, gla_b_f, gla_w2_b, gla_b_b, gla_norm_g,
           swa_sink, swa_norm_g, hy_conv_w, hy_conv_b, hy_w1, hy_b1, hy_freq, hy_w2, hy_b2, hy_w3, hy_skip,
           hy_norm_g, w_out, b_out, ln1_g, ln1_b, router_wg, router_bg, router_we, router_be,
           exp_w_gate, exp_w_up, exp_w_down, ln2_g, ln2_b):
    bsz, seq_in, d = x.shape
    depth = w_in.shape[0]
    seq = seq_in + N_META
    t = bsz * seq
    alpha = (2.0 * depth) ** 0.25

    o_gates, o_r, o_swa, o_hy = IN_GATES, IN_R, IN_SWA, IN_HY
    pad_cols = A_COLS - (o_swa)
    w_a, w_b, w_c = regroup_in_weights(w_in)
    b_a = jnp.concatenate([b_in[:, :o_gates], b_in[:, o_r:o_swa], b_in[:, o_gates:o_r],
                           jnp.zeros((depth, pad_cols), b_in.dtype)], axis=-1)
    q_scale = jnp.concatenate([jnp.full((SWA_WIDTH,), SWA_Q_SCALE, F32), jnp.ones((o_hy - o_swa - SWA_WIDTH,), F32)])
    b_b = b_in[:, o_swa:o_hy] * q_scale
    w_out_b = w_out.astype(BF16)
    w2f = jnp.zeros((depth, LANES, gla_w2_f.shape[2]), F32).at[:, :GLA_RANK].set(gla_w2_f)
    w2b = jnp.zeros((depth, LANES, gla_w2_b.shape[2]), F32).at[:, GLA_RANK:2 * GLA_RANK].set(gla_w2_b)
    n_r = N_GROUPS + N_EXPERTS
    w_router = jnp.zeros((depth, d, LANES), F32).at[:, :, :N_GROUPS].set(router_wg)
    w_router = w_router.at[:, :, N_GROUPS:n_r].set(router_we).astype(BF16)
    b_router = jnp.zeros((depth, 1, LANES), F32).at[:, 0, :N_GROUPS].set(router_bg)
    b_router = b_router.at[:, 0, N_GROUPS:n_r].set(router_be)
    d_exp = exp_w_gate.shape[-1]
    wg_all = exp_w_gate.reshape(depth * N_EXPERTS, d, d_exp)
    wu_all = exp_w_up.reshape(depth * N_EXPERTS, d, d_exp)
    wd_all = exp_w_down.reshape(depth * N_EXPERTS, d_exp, d)

    z_np, rates_np, cos_np, sin_np = _hyena_tables(seq)
    cos_t = jnp.asarray(cos_np).astype(BF16)
    sin_t = jnp.asarray(sin_np).astype(BF16)
    w1p = jnp.zeros((depth, LANES, HY_FFN), F32).at[:, :HY_EMB].set(hy_w1)
    h_sum, h_diff = hyena_filters(jnp.asarray(z_np), jnp.asarray(rates_np), w1p, hy_b1.reshape(depth, 1, HY_FFN),
                                  hy_freq, hy_w2, hy_b2.reshape(depth, 1, HY_FFN), hy_w3)
    k_cos, k_sin = hyena_spectrum(cos_t, sin_t, h_sum, h_diff)

    n_steps = (2 * t + N_EXPERTS * (MOE_TILE - 1)) // MOE_TILE + 2

    tokens = jnp.concatenate([jnp.broadcast_to(meta.astype(x.dtype)[None], (bsz, N_META, d)), x], axis=1)
    h, hb = ln_rows(tokens.reshape(t, d), emb_ln_g, emb_ln_b)
    b_c = b_in[:, o_hy:]
    u_a = project(hb, w_a, 0, b_a[0].reshape(1, -1), F32)
    u_b = project(hb, w_b, 0, b_b[0].reshape(1, -1), BF16)
    u_c = project(hb, w_c, 0, b_c[0].reshape(1, -1), F32)
    for l in range(depth):
        u_a, u_b, u_c = (u.reshape(bsz, seq, -1) for u in (u_a, u_b, u_c))
        y_a = gla_branch(u_a, w2f[l], gla_b_f[l], w2b[l], gla_b_b[l], gla_norm_g[l])
        y_b = swa_branch(u_b, swa_sink[l], swa_norm_g[l])
        y_c = hyena_branch(u_c, hy_conv_w[l], hy_conv_b[l].reshape(1, -1), cos_t, sin_t, k_cos, k_sin, l,
                           hy_skip[l], hy_norm_g[l])
        h1, routed = out_projection(alpha, y_a.reshape(t, -1), y_b.reshape(t, -1), y_c.reshape(t, -1), h,
                                    w_out_b, l, b_out[l], ln1_g[l], ln1_b[l], w_router, b_router)
        expert_ids = routed[:, 0:2].astype(jnp.int32)
        row_token, row_dest, tile_expert, n_used, run_start, next_expert = dispatch_plan(expert_ids, n_steps)
        base = l * N_EXPERTS
        y_assign = expert_mlps(h1, wg_all, wu_all, wd_all, tile_expert + base, n_used, row_token, row_dest,
                               run_start, jnp.where(next_expert >= 0, next_expert + base, -1))
        if l + 1 < depth:
            h, u_a, u_b, u_c = combine_ln_project(
                alpha, y_assign, h1, routed, ln2_g[l], ln2_b[l], w_a, w_b, w_c, l + 1,
                b_a[l + 1].reshape(1, -1), b_b[l + 1].reshape(1, -1), b_c[l + 1].reshape(1, -1))
    return combine_ln_final(alpha, y_assign, h1, routed, ln2_g[depth - 1], ln2_b[depth - 1], bsz, seq)
```

```python
import functools
import math

import jax
import jax.numpy as jnp
import numpy as np
from jax import lax
from jax.experimental import pallas as pl
from jax.experimental.pallas import tpu as pltpu

F32 = jnp.float32
BF16 = jnp.bfloat16
HIGHEST = lax.Precision.HIGHEST

D_MODEL = 2048
N_META = 16
GLA_HEADS = 4
GLA_DK = 64
GLA_DV = 128
GLA_WIDTH = GLA_HEADS * GLA_DV
GLA_RANK = 16
GLA_TAU = 16.0
GLA_CHUNK = 64
GLA_UNROLL = 11
SWA_HEADS = 8
SWA_KV_HEADS = 2
SWA_GROUP = SWA_HEADS // SWA_KV_HEADS
SWA_HEAD_DIM = 128
SWA_WIDTH = SWA_HEADS * SWA_HEAD_DIM
WINDOW = 128
HY_WIDTH = 512
HY_BANDS = 16
HY_EMB = 2 * HY_BANDS + 1
HY_FFN = 64
HY_FAST_PCT = 0.3
HY_SLOW_PCT = 1.5
HY_TARGET = 1e-2
N_GROUPS = 4
EXPERTS_PER_GROUP = 4
N_EXPERTS = N_GROUPS * EXPERTS_PER_GROUP
D_EXPERT = 1024
LN_EPS = 1e-5
RMS_EPS = 1e-6
NEG = -1e30

LANES = 128
A_COLS = 1664
GATE_COL0 = 1536
ROW_TILE = 384
SORT_VALUE_RANGE = 1 << 16
FINAL_TILE = 256
FUSED_TILE = 192
MOE_TILE = 256
VMEM_LIMIT = 56 * 1024 * 1024


def _cparams(sem):
    return pltpu.CompilerParams(dimension_semantics=sem, vmem_limit_bytes=VMEM_LIMIT)


def _resident(shape, index_map):
    return pl.BlockSpec(shape, index_map, pipeline_mode=pl.Buffered(1))


def _dot(a, b, **kw):
    return jnp.dot(a, b, preferred_element_type=F32, **kw)


def _dot_nt(a, b):
    return lax.dot_general(a, b, (((1,), (1,)), ((), ())), preferred_element_type=F32)


def _ln(x, g, b):
    mu = jnp.mean(x, axis=-1, keepdims=True)
    xc = x - mu
    var = jnp.mean(xc * xc, axis=-1, keepdims=True)
    return xc * lax.rsqrt(var + LN_EPS) * g + b


def _ln_rows_kernel(x_ref, g_ref, b_ref, o_ref, ob_ref):
    y = _ln(x_ref[...], g_ref[...], b_ref[...])
    o_ref[...] = y
    ob_ref[...] = y.astype(BF16)


def ln_rows(x, g, b):
    t, d = x.shape
    row = pl.BlockSpec((ROW_TILE, d), lambda i: (i, 0))
    vec = pl.BlockSpec((1, d), lambda i: (0, 0))
    return pl.pallas_call(
        _ln_rows_kernel,
        grid=(t // ROW_TILE,),
        in_specs=[row, vec, vec],
        out_specs=[row, row],
        out_shape=[jax.ShapeDtypeStruct((t, d), F32), jax.ShapeDtypeStruct((t, d), BF16)],
        compiler_params=_cparams(("parallel",)),
        name="ln_rows",
    )(x, g.reshape(1, d), b.reshape(1, d))


IN_GATES, IN_R, IN_SWA, IN_HY = 1024, 1056, 1568, 3104


def _regroup_kernel(w_ref, wa_ref, wb_ref, wc_ref):
    w = w_ref[0]
    rows = w.shape[0]
    pad = jnp.zeros((rows, wa_ref.shape[2] - IN_SWA), F32)
    wa = jnp.concatenate([w[:, :IN_GATES], w[:, IN_R:IN_SWA], w[:, IN_GATES:IN_R], pad], axis=1)
    wa_ref[0] = wa.astype(BF16)
    q_end = IN_SWA + SWA_WIDTH
    wb_ref[0] = jnp.concatenate([w[:, IN_SWA:q_end] * SWA_Q_SCALE, w[:, q_end:IN_HY]], axis=1).astype(BF16)
    wc_ref[0] = w[:, IN_HY:].astype(BF16)


def regroup_in_weights(w_in):
    depth, d, cols = w_in.shape
    rows = 256
    nb, nc = IN_HY - IN_SWA, cols - IN_HY

    def out(n):
        return pl.BlockSpec((1, rows, n), lambda l, i: (l, i, 0))

    return pl.pallas_call(
        _regroup_kernel,
        grid=(depth, d // rows),
        in_specs=[pl.BlockSpec((1, rows, cols), lambda l, i: (l, i, 0))],
        out_specs=[out(A_COLS), out(nb), out(nc)],
        out_shape=[jax.ShapeDtypeStruct((depth, d, A_COLS), BF16), jax.ShapeDtypeStruct((depth, d, nb), BF16),
                   jax.ShapeDtypeStruct((depth, d, nc), BF16)],
        compiler_params=_cparams(("parallel", "parallel")),
        name="regroup_in_weights",
    )(w_in)

def _proj_kernel(x_ref, w_ref, b_ref, o_ref):
    o_ref[...] = (_dot(x_ref[...], w_ref[0]) + b_ref[...]).astype(o_ref.dtype)


def project(x, w, layer, b, out_dtype):
    t, k = x.shape
    n = w.shape[2]
    return pl.pallas_call(
        _proj_kernel,
        grid=(t // ROW_TILE,),
        in_specs=[pl.BlockSpec((ROW_TILE, k), lambda i: (i, 0)),
                  _resident((1, k, n), lambda i: (layer, 0, 0)),
                  pl.BlockSpec((1, n), lambda i: (0, 0))],
        out_specs=pl.BlockSpec((ROW_TILE, n), lambda i: (i, 0)),
        out_shape=jax.ShapeDtypeStruct((t, n), out_dtype),
        compiler_params=_cparams(("parallel",)),
        name="in_proj",
    )(x, w, b)


def _log_sigmoid(x):
    return jnp.minimum(x, 0.0) - jnp.log(1.0 + jnp.exp(-jnp.abs(x)))


def _chunk_scan(x, reverse):
    c = x.shape[0]
    idx = lax.broadcasted_iota(jnp.int32, x.shape, 0)
    s = 1
    while s < c:
        if reverse:
            x = x + jnp.where(idx < c - s, pltpu.roll(x, c - s, axis=0), 0.0)
        else:
            x = x + jnp.where(idx >= s, pltpu.roll(x, s, axis=0), 0.0)
        s *= 2
    return x


def _gla_kernel(q_ref, k_ref, v_ref, r_ref, gt_ref, w2f_ref, bf_ref, w2b_ref, bb_ref, ng_ref, o_ref,
                qs, ks, vs, lfs, lbs, acc, acc_b, st):
    seq = q_ref.shape[1]
    seq_pad = qs.shape[0]
    n_chunks = seq_pad // GLA_CHUNK
    tail = seq_pad - seq
    c = GLA_CHUNK

    gates = gt_ref[0]
    log_f = _log_sigmoid(_dot(gates, w2f_ref[...], precision=HIGHEST) + bf_ref[...]) * (1.0 / GLA_TAU)
    log_b = _log_sigmoid(_dot(gates, w2b_ref[...], precision=HIGHEST) + bb_ref[...]) * (1.0 / GLA_TAU)
    for dst, src in ((qs, q_ref[0]), (ks, k_ref[0]), (vs, v_ref[0]), (lfs, log_f), (lbs, log_b)):
        dst[0:seq, :] = src
        dst[seq:seq_pad, :] = jnp.zeros((tail, dst.shape[1]), F32)

    lane = lax.broadcasted_iota(jnp.int32, (c, 2 * GLA_DK), 1)
    row = lax.broadcasted_iota(jnp.int32, (c, c), 0)
    col = lax.broadcasted_iota(jnp.int32, (c, c), 1)
    scale = GLA_DK ** -0.5

    st[...] = jnp.zeros(st.shape, F32)

    def chunk(n, log_ref, reverse, out_ref):
        keep = (col > row) if reverse else (col <= row)
        r0 = pl.multiple_of(n * c, c)
        cum = _chunk_scan(log_ref[pl.ds(r0, c), :], reverse)
        tot = cum[0:1, :] if reverse else cum[c - 1:c, :]
        q = qs[pl.ds(r0, c), :]
        k = ks[pl.ds(r0, c), :]
        q_dec = q * jnp.exp(cum) * scale
        k_inv = (k * jnp.exp(-cum)).astype(BF16)
        k_dec = (k * jnp.exp(tot - cum)).astype(BF16)
        decay = jnp.exp(tot)
        for h in range(2):
            slot = 2 * int(reverse) + h
            head_lanes = (lane < GLA_DK) if h == 0 else (lane >= GLA_DK)
            qm = jnp.where(head_lanes, q_dec, 0.0).astype(BF16)
            att = jnp.where(keep, _dot_nt(qm, k_inv), 0.0)
            vh = vs[pl.ds(r0, c), h * GLA_DV:(h + 1) * GLA_DV]
            state = st[slot]
            out_ref[pl.ds(r0, c), h * GLA_DV:(h + 1) * GLA_DV] = (
                _dot(att.astype(BF16), vh.astype(BF16)) + _dot_nt(qm, state.astype(BF16)))
            st[slot] = state * decay + _dot(vh.T.astype(BF16), k_dec)

    def body(i, carry):
        chunk(i, lfs, False, acc)
        chunk(n_chunks - 1 - i, lbs, True, acc_b)
        return carry

    lax.fori_loop(0, n_chunks, body, 0, unroll=GLA_UNROLL)

    r = r_ref[0]
    gate = r * (1.0 / (1.0 + jnp.exp(-r)))
    for h in range(2):
        sl = slice(h * GLA_DV, (h + 1) * GLA_DV)
        o = acc[0:seq, sl] + acc_b[0:seq, sl]
        y = o * lax.rsqrt(jnp.mean(o * o, axis=-1, keepdims=True) + RMS_EPS) * ng_ref[:, sl]
        o_ref[0, :, sl] = (y * gate[:, sl]).astype(o_ref.dtype)


def gla_branch(u_a, w2f, b_f, w2b, b_b, norm_g):
    bsz, seq, _ = u_a.shape
    seq_pad = -(-seq // GLA_CHUNK) * GLA_CHUNK
    pair = 2 * GLA_DK

    def col(width, first_block):
        return pl.BlockSpec((1, seq, width), lambda b, p: (b, 0, first_block + p))

    vec128 = pl.BlockSpec((1, pair), lambda b, p: (0, p))
    return pl.pallas_call(
        _gla_kernel,
        grid=(bsz, GLA_HEADS // 2),
        in_specs=[col(pair, 0), col(pair, 2), col(2 * GLA_DV, 2), col(2 * GLA_DV, 4),
                  pl.BlockSpec((1, seq, LANES), lambda b, p: (b, 0, GATE_COL0 // LANES)),
                  pl.BlockSpec((LANES, pair), lambda b, p: (0, p)), vec128,
                  pl.BlockSpec((LANES, pair), lambda b, p: (0, p)), vec128,
                  pl.BlockSpec((1, 2 * GLA_DV), lambda b, p: (0, p))],
        out_specs=pl.BlockSpec((1, seq, 2 * GLA_DV), lambda b, p: (b, 0, p)),
        out_shape=jax.ShapeDtypeStruct((bsz, seq, GLA_WIDTH), BF16),
        scratch_shapes=[pltpu.VMEM((seq_pad, pair), F32), pltpu.VMEM((seq_pad, pair), F32),
                        pltpu.VMEM((seq_pad, 2 * GLA_DV), F32), pltpu.VMEM((seq_pad, pair), F32),
                        pltpu.VMEM((seq_pad, pair), F32), pltpu.VMEM((seq_pad, 2 * GLA_DV), F32),
                        pltpu.VMEM((seq_pad, 2 * GLA_DV), F32), pltpu.VMEM((4, GLA_DV, pair), F32)],
        compiler_params=_cparams(("parallel", "parallel")),
        name="gla",
    )(u_a, u_a, u_a, u_a, u_a, w2f, b_f.reshape(1, -1), w2b, b_b.reshape(1, -1), norm_g.reshape(1, -1))


LOG2E = 1.4426950408889634
SWA_Q_SCALE = SWA_HEAD_DIM ** -0.5 * LOG2E


def _swa_slope2(h):
    return LOG2E * 2.0 ** (-8.0 * (h + 1) / SWA_HEADS)


def _swa_kernel(sink_ref, q_ref, k_ref, v_ref, ng_ref, o_ref, band_bias, meta_bias):
    seq = q_ref.shape[1]
    blk = WINDOW
    n_blocks = (seq - N_META) // blk
    hd = SWA_HEAD_DIM
    grp = SWA_GROUP

    def stacked(fn, nq):
        row = lax.broadcasted_iota(jnp.int32, (grp * nq, 1), 0)
        col = jnp.full((grp * nq, 1), fn(grp - 1), F32)
        for g in range(grp - 2, -1, -1):
            col = jnp.where(row < (g + 1) * nq, fn(g), col)
        return col

    def band_bias_tile(kv, nq, nk, delta):
        qi = lax.broadcasted_iota(jnp.int32, (grp * nq, nk), 0) & (nq - 1)
        ki = lax.broadcasted_iota(jnp.int32, (grp * nq, nk), 1)
        dist = jnp.abs(qi - ki + delta)
        slope = stacked(lambda g: _swa_slope2(kv * grp + g), nq)
        return jnp.where(dist <= WINDOW, -slope * dist.astype(F32), NEG)

    def meta_bias_tile(kv, nq, among_meta):
        qi = lax.broadcasted_iota(jnp.int32, (grp * nq, blk), 0) & (nq - 1)
        ki = lax.broadcasted_iota(jnp.int32, (grp * nq, blk), 1)
        rel = jnp.abs(qi - ki) if among_meta else qi - ki
        slope = stacked(lambda g: _swa_slope2(kv * grp + g), nq)
        return jnp.where(ki < N_META, -slope * rel.astype(F32), NEG)

    def attend(r0, nq, w0, nk, band_fn, meta_fn):
        outs = [None] * SWA_HEADS
        for kv in range(SWA_KV_HEADS):
            ks = slice(kv * hd, (kv + 1) * hd)
            q4 = jnp.concatenate([q_ref[0, pl.ds(r0, nq), (kv * grp + g) * hd:(kv * grp + g + 1) * hd]
                                  for g in range(grp)], axis=0)
            s_band = _dot_nt(q4, k_ref[0, pl.ds(w0, nk), ks]) + band_fn(kv)
            s_meta = _dot_nt(q4, k_ref[0, 0:blk, ks]) + meta_fn(kv)
            sink = stacked(lambda g: sink_ref[kv * grp + g] * LOG2E, nq)
            m = jnp.maximum(jnp.maximum(jnp.max(s_band, axis=-1, keepdims=True),
                                        jnp.max(s_meta, axis=-1, keepdims=True)), sink)
            p_band = jnp.exp2(s_band - m)
            p_meta = jnp.exp2(s_meta - m)
            den = (jnp.sum(p_band, axis=-1, keepdims=True) + jnp.sum(p_meta, axis=-1, keepdims=True)
                   + jnp.exp2(sink - m))
            o4 = (_dot(p_band.astype(BF16), v_ref[0, pl.ds(w0, nk), ks])
                  + _dot(p_meta.astype(BF16), v_ref[0, 0:blk, ks])) * (1.0 / den)
            for g in range(grp):
                outs[kv * grp + g] = o4[g * nq:(g + 1) * nq]
        ssq = outs[0] * outs[0]
        for o in outs[1:]:
            ssq = ssq + o * o
        inv = lax.rsqrt(jnp.sum(ssq, axis=-1, keepdims=True) * (1.0 / SWA_WIDTH) + RMS_EPS)
        for h in range(SWA_HEADS):
            cols = slice(h * hd, (h + 1) * hd)
            o_ref[0, pl.ds(r0, nq), cols] = (outs[h] * inv * ng_ref[:, cols]).astype(o_ref.dtype)

    for kv in range(SWA_KV_HEADS):
        band_bias[kv] = band_bias_tile(kv, blk, 3 * blk, blk)
        meta_bias[kv] = meta_bias_tile(kv, blk, False)

    def meta_fn_at(r0):
        def fn(kv):
            slope = stacked(lambda g: _swa_slope2(kv * grp + g), blk)
            return meta_bias[kv] - slope * r0
        return fn

    last_r0 = N_META + (n_blocks - 1) * blk
    attend(0, N_META, N_META, blk, lambda kv: band_bias_tile(kv, N_META, blk, -N_META),
           lambda kv: meta_bias_tile(kv, N_META, True))
    attend(N_META, blk, N_META, 3 * blk, lambda kv: band_bias_tile(kv, blk, 3 * blk, 0), meta_fn_at(float(N_META)))
    attend(last_r0, blk, last_r0 - 2 * blk, 3 * blk, lambda kv: band_bias_tile(kv, blk, 3 * blk, 2 * blk),
           meta_fn_at(float(last_r0)))

    def body(j, carry):
        r0 = pl.multiple_of(N_META + j * blk, N_META)
        attend(r0, blk, pl.multiple_of(r0 - blk, N_META), 3 * blk, lambda kv: band_bias[kv],
               meta_fn_at(r0.astype(F32)))
        return carry

    lax.fori_loop(1, n_blocks - 1, body, 0)


def swa_branch(u_b, sink, norm_g):
    bsz, seq, _ = u_b.shape
    kvw = SWA_KV_HEADS * SWA_HEAD_DIM
    rows = SWA_GROUP * WINDOW
    return pl.pallas_call(
        _swa_kernel,
        grid=(bsz,),
        in_specs=[pl.BlockSpec(memory_space=pltpu.SMEM),
                  pl.BlockSpec((1, seq, SWA_WIDTH), lambda b: (b, 0, 0)),
                  pl.BlockSpec((1, seq, kvw), lambda b: (b, 0, SWA_WIDTH // kvw)),
                  pl.BlockSpec((1, seq, kvw), lambda b: (b, 0, SWA_WIDTH // kvw + 1)),
                  pl.BlockSpec((1, SWA_WIDTH), lambda b: (0, 0))],
        out_specs=pl.BlockSpec((1, seq, SWA_WIDTH), lambda b: (b, 0, 0)),
        out_shape=jax.ShapeDtypeStruct((bsz, seq, SWA_WIDTH), BF16),
        scratch_shapes=[pltpu.VMEM((SWA_KV_HEADS, rows, 3 * WINDOW), F32),
                        pltpu.VMEM((SWA_KV_HEADS, rows, WINDOW), F32)],
        compiler_params=_cparams(("parallel",)),
        name="swa",
    )(sink, u_b, u_b, u_b, norm_g.reshape(1, -1))


@functools.lru_cache(maxsize=None)
def _hyena_tables(seq):
    t = np.linspace(0.0, 1.0, seq, dtype=np.float32)[:, None]
    w = (np.float32(2.0 * math.pi) * np.arange(seq, dtype=np.float32)[:, None] / np.float32(seq)).astype(np.float32)
    bands = np.linspace(1e-4, HY_BANDS - 1, HY_BANDS, dtype=np.float32)
    arg = (bands * w).astype(np.float32).astype(np.float64)
    z = np.concatenate([t.astype(np.float64), np.cos(arg), -np.sin(arg)], axis=-1)
    z_pad = np.zeros((seq, LANES), np.float32)
    z_pad[:, :HY_EMB] = z.astype(np.float32)
    max_decay = math.log(HY_TARGET) / HY_FAST_PCT
    min_decay = math.log(HY_TARGET) / HY_SLOW_PCT
    rates = np.abs(np.linspace(min_decay, max_decay, HY_WIDTH, dtype=np.float32))[None, :]
    n = 2 * seq - 1
    fj = (np.arange(seq, dtype=np.int64)[:, None] * np.arange(seq, dtype=np.int64)[None, :]) % n
    ang = fj.astype(np.float64) * (2.0 * math.pi / n)
    return z_pad, rates.astype(np.float32), np.cos(ang).astype(np.float32), np.sin(ang).astype(np.float32)


def _hy_filter_kernel(z_ref, rate_ref, w1_ref, b1_ref, fr_ref, w2_ref, b2_ref, w3_ref, hs_ref, hd_ref):
    z = z_ref[...]
    h = jnp.sin(fr_ref[0, 0:1, :] * (_dot(z, w1_ref[0], precision=HIGHEST) + b1_ref[0]))
    h = jnp.sin(fr_ref[0, 1:2, :] * (_dot(h, w2_ref[0], precision=HIGHEST) + b2_ref[0]))
    h = _dot(h, w3_ref[0], precision=HIGHEST)
    window = jnp.exp(-z[:, 0:1] * rate_ref[...])
    h_fwd = h[:, :HY_WIDTH] * window
    h_bwd = h[:, HY_WIDTH:] * window
    first = lax.broadcasted_iota(jnp.int32, h_bwd.shape, 0) == 0
    h_bwd = jnp.where(first, 0.0, h_bwd)
    hs_ref[0] = h_fwd + h_bwd
    hd_ref[0] = h_bwd - h_fwd


def hyena_filters(z, rates, w1p, b1, freq, w2, b2, w3):
    depth = w1p.shape[0]
    seq = z.shape[0]

    def per_layer(*shape):
        return pl.BlockSpec((1,) + shape, lambda l: (l,) + (0,) * len(shape))

    out = jax.ShapeDtypeStruct((depth, seq, HY_WIDTH), F32)
    return pl.pallas_call(
        _hy_filter_kernel,
        grid=(depth,),
        in_specs=[pl.BlockSpec((seq, LANES), lambda l: (0, 0)), pl.BlockSpec((1, HY_WIDTH), lambda l: (0, 0)),
                  per_layer(LANES, HY_FFN), per_layer(1, HY_FFN), per_layer(2, HY_FFN),
                  per_layer(HY_FFN, HY_FFN), per_layer(1, HY_FFN), per_layer(HY_FFN, 2 * HY_WIDTH)],
        out_specs=[per_layer(seq, HY_WIDTH), per_layer(seq, HY_WIDTH)],
        out_shape=[out, out],
        compiler_params=_cparams(("parallel",)),
        name="hyena_filters",
    )(z, rates, w1p, b1, freq, w2, b2, w3)


def _split_bf16(x):
    hi = x.astype(BF16)
    return hi, (x - hi.astype(F32)).astype(BF16)


def _hy_spectrum_kernel(cos_ref, sin_ref, hs_ref, hd_ref, kc_ref, ks_ref):
    seq = cos_ref.shape[0]
    n = 2 * seq - 1
    f = lax.broadcasted_iota(jnp.int32, (seq, 1), 0)
    weight = jnp.where(f == 0, 1.0 / n, 2.0 / n)
    s_hi, s_lo = _split_bf16(hs_ref[0])
    d_hi, d_lo = _split_bf16(hd_ref[0])
    kc_ref[0] = (_dot(cos_ref[...], s_hi) + _dot(cos_ref[...], s_lo)) * weight
    ks_ref[0] = (_dot(sin_ref[...], d_hi) + _dot(sin_ref[...], d_lo)) * weight


def hyena_spectrum(cos_t, sin_t, h_sum, h_diff):
    depth, seq, _ = h_sum.shape
    half = HY_WIDTH // 2
    table = _resident((seq, seq), lambda l, c: (0, 0))
    blk = pl.BlockSpec((1, seq, half), lambda l, c: (l, 0, c))
    out = jax.ShapeDtypeStruct((depth, seq, HY_WIDTH), F32)
    return pl.pallas_call(
        _hy_spectrum_kernel,
        grid=(depth, 2),
        in_specs=[table, table, blk, blk],
        out_specs=[blk, blk],
        out_shape=[out, out],
        compiler_params=_cparams(("parallel", "parallel")),
        name="hyena_spectrum",
    )(cos_t, sin_t, h_sum, h_diff)


def _short_conv(u, w, b):
    seq = u.shape[0]
    row = lax.broadcasted_iota(jnp.int32, u.shape, 0)
    prev = jnp.where(row == 0, 0.0, pltpu.roll(u, 1, axis=0))
    nxt = jnp.where(row == seq - 1, 0.0, pltpu.roll(u, seq - 1, axis=0))
    return prev * w[0:1, :] + u * w[1:2, :] + nxt * w[2:3, :] + b


def _hyena_kernel(x0_ref, x1_ref, v_ref, w0_ref, w1_ref, wv_ref, b0_ref, b1_ref, bv_ref,
                  cos_ref, sin_ref, kc_ref, ks_ref, skip_ref, ng_ref, o_ref, acc):
    c = pl.program_id(1)
    half = x0_ref.shape[2]
    x0 = _short_conv(x0_ref[0], w0_ref[...], b0_ref[...])
    x1 = _short_conv(x1_ref[0], w1_ref[...], b1_ref[...])
    v = _short_conv(v_ref[0], wv_ref[...], bv_ref[...])
    g = x1 * v
    gb = g.astype(BF16)
    uc = _dot(cos_ref[...], gb)
    us = _dot(sin_ref[...], gb)
    kc = kc_ref[0]
    ks = ks_ref[0]
    pr = (uc * kc + us * ks).astype(BF16)
    pi = (uc * ks - us * kc).astype(BF16)
    y = _dot(cos_ref[...], pr) - _dot(sin_ref[...], pi) + g * skip_ref[...]
    y = x0 * y

    @pl.when(c == 0)
    def _():
        acc[:, 0:half] = y

    @pl.when(c == 1)
    def _():
        acc[:, half:2 * half] = y
        full = acc[...]
        o_ref[0] = (full * lax.rsqrt(jnp.mean(full * full, axis=-1, keepdims=True) + RMS_EPS)
                    * ng_ref[...]).astype(o_ref.dtype)


def hyena_branch(u_c, conv_w, conv_b, cos_t, sin_t, k_cos, k_sin, layer, skip, norm_g):
    bsz, seq, _ = u_c.shape
    half = HY_WIDTH // 2

    def stream(first_block):
        return pl.BlockSpec((1, seq, half), lambda b, c: (b, 0, first_block + c))

    def wcol(rows, first_block):
        return pl.BlockSpec((rows, half), lambda b, c: (0, first_block + c))

    table = _resident((seq, seq), lambda b, c: (0, 0))
    spec = pl.BlockSpec((1, seq, half), lambda b, c: (layer, 0, c))
    return pl.pallas_call(
        _hyena_kernel,
        grid=(bsz, 2),
        in_specs=[stream(0), stream(2), stream(4), wcol(3, 0), wcol(3, 2), wcol(3, 4),
                  wcol(1, 0), wcol(1, 2), wcol(1, 4), table, table, spec, spec, wcol(1, 0),
                  pl.BlockSpec((1, HY_WIDTH), lambda b, c: (0, 0))],
        out_specs=pl.BlockSpec((1, seq, HY_WIDTH), lambda b, c: (b, 0, 0)),
        out_shape=jax.ShapeDtypeStruct((bsz, seq, HY_WIDTH), BF16),
        scratch_shapes=[pltpu.VMEM((seq, HY_WIDTH), F32)],
        compiler_params=_cparams(("parallel", "arbitrary")),
        name="hyena",
    )(u_c, u_c, u_c, conv_w, conv_w, conv_w, conv_b, conv_b, conv_b, cos_t, sin_t, k_cos, k_sin,
      skip.reshape(1, -1), norm_g.reshape(1, -1))


def _first_index_of_max(x, valid, lane):
    m = jnp.max(jnp.where(valid, x, NEG), axis=-1, keepdims=True)
    idx = jnp.min(jnp.where(valid & (x == m), lane, float(LANES)), axis=-1, keepdims=True)
    return m, idx


def _route(x):
    lane = lax.broadcasted_iota(jnp.int32, x.shape, 1).astype(F32)
    is_group = lane < N_GROUPS
    gm, g_idx = _first_index_of_max(x, is_group, lane)
    g_top = 1.0 / jnp.sum(jnp.where(is_group, jnp.exp(x - gm), 0.0), axis=-1, keepdims=True)
    lo = N_GROUPS + g_idx * EXPERTS_PER_GROUP
    in_group = (lane >= lo) & (lane < lo + EXPERTS_PER_GROUP)
    m1, i1 = _first_index_of_max(x, in_group, lane)
    m2, i2 = _first_index_of_max(x, in_group & (lane != i1), lane)
    e2 = jnp.exp(m2 - m1)
    w1 = g_top / (1.0 + e2)
    w2 = g_top * e2 / (1.0 + e2)
    return jnp.where(lane == 0, i1 - N_GROUPS,
                     jnp.where(lane == 1, i2 - N_GROUPS,
                               jnp.where(lane == 2, w1, jnp.where(lane == 3, w2, 0.0))))


def _outproj_kernel(alpha, ya_ref, yb_ref, yc_ref, h_ref, w_ref, b_ref, g_ref, be_ref, wr_ref, br_ref,
                    h1_ref, rt_ref, z_scr):
    i = pl.program_id(0)
    last = pl.num_programs(0) - 1
    na = ya_ref.shape[1]
    nb = yb_ref.shape[1]

    def project(slot):
        mix = (_dot(ya_ref[...], w_ref[0, 0:na, :]) + _dot(yb_ref[...], w_ref[0, na:na + nb, :])
               + _dot(yc_ref[...], w_ref[0, na + nb:, :]) + b_ref[...])
        z_scr[slot] = alpha * h_ref[...] + mix

    def finish(slot):
        h1 = _ln(z_scr[slot], g_ref[...], be_ref[...])
        h1_ref[...] = h1
        rt_ref[...] = _route(_dot(h1.astype(BF16), wr_ref[0]) + br_ref[0])

    @pl.when(i == 0)
    def _():
        project(0)

    for cur in range(2):
        @pl.when((i % 2 == cur) & (i > 0) & (i < last))
        def _():
            finish(1 - cur)
            project(cur)

        @pl.when((i % 2 == cur) & (i == last))
        def _():
            finish(1 - cur)


def out_projection(alpha, y_a, y_b, y_c, h, w, layer, b, g, be, w_router, b_router):
    t, d = h.shape
    n_tiles = t // ROW_TILE

    def row_in(width):
        return pl.BlockSpec((ROW_TILE, width), lambda i: (jnp.minimum(i, n_tiles - 1), 0))

    def row_out(width):
        return pl.BlockSpec((ROW_TILE, width), lambda i: (jnp.maximum(i - 1, 0), 0))

    def vec():
        return pl.BlockSpec((1, d), lambda i: (0, 0))

    return pl.pallas_call(
        functools.partial(_outproj_kernel, alpha),
        grid=(n_tiles + 1,),
        in_specs=[row_in(y_a.shape[1]), row_in(y_b.shape[1]), row_in(y_c.shape[1]), row_in(d),
                  _resident((1, d, d), lambda i: (layer, 0, 0)), vec(), vec(), vec(),
                  _resident((1, d, LANES), lambda i: (layer, 0, 0)),
                  pl.BlockSpec((1, 1, LANES), lambda i: (layer, 0, 0))],
        out_specs=[row_out(d), row_out(LANES)],
        out_shape=[jax.ShapeDtypeStruct((t, d), F32), jax.ShapeDtypeStruct((t, LANES), F32)],
        scratch_shapes=[pltpu.VMEM((2, ROW_TILE, d), F32)],
        compiler_params=_cparams(("arbitrary",)),
        name="out_proj_ln",
    )(y_a, y_b, y_c, h, w, b.reshape(1, d), g.reshape(1, d), be.reshape(1, d), w_router, b_router)


def dispatch_plan(expert_ids, n_steps):
    t = expert_ids.shape[0]
    tm = MOE_TILE
    flat = expert_ids.reshape(-1)
    experts = jnp.arange(N_EXPERTS, dtype=jnp.int32)
    counts = jnp.sum((flat[:, None] == experts[None, :]).astype(jnp.int32), axis=0)
    padded = (counts + tm - 1) // tm * tm
    ends = jnp.cumsum(padded)
    pad_needed = jnp.arange(tm, dtype=jnp.int32)[None, :] < (padded - counts)[:, None]
    pad_keys = jnp.where(pad_needed, 2 * experts[:, None] + 1, 2 * N_EXPERTS).reshape(-1)
    keys = jnp.concatenate([2 * flat, pad_keys])
    vals = jnp.concatenate([jnp.arange(2 * t, dtype=jnp.int32), jnp.full((N_EXPERTS * tm,), 2 * t, jnp.int32)])
    assert 2 * t < SORT_VALUE_RANGE
    ordered = lax.sort(keys * SORT_VALUE_RANGE + vals) % SORT_VALUE_RANGE
    ordered = jnp.pad(ordered, (tm, (n_steps + 1) * tm - tm - ordered.shape[0]), constant_values=2 * t)
    lane = jnp.arange(ordered.shape[0], dtype=jnp.int32) % tm
    row_assign = jnp.where(ordered >= 2 * t, 2 * t + lane, ordered)
    plane = t + MOE_TILE // 2
    row_token = jnp.minimum(row_assign >> 1, t - 1)
    row_dest = (row_assign >> 1) + (row_assign & 1) * plane
    n_used = (ends[-1] // MOE_TILE).astype(jnp.int32)
    step = jnp.arange(n_steps, dtype=jnp.int32)
    tile_start = jnp.minimum(step, n_used - 1) * MOE_TILE
    tile_expert = jnp.sum((tile_start[:, None] >= ends[None, :]).astype(jnp.int32), axis=1)
    run_start = (step < n_used) & ((step == 0) | (tile_expert != jnp.roll(tile_expert, 1)))
    after = ends[tile_expert] // MOE_TILE
    next_expert = jnp.where(after < n_used, tile_expert[jnp.minimum(after, n_steps - 1)], -1)
    return (row_token, row_dest, tile_expert.astype(jnp.int32), n_used.reshape(1), run_start.astype(jnp.int32),
            next_expert.astype(jnp.int32))


def _moe_kernel(te_ref, nu_ref, rt_ref, rd_ref, rs_ref, ne_ref, x_hbm, wg_hbm, wu_hbm, wd_hbm, y_hbm,
                xbuf, ybuf, land_g, land_u, land_d, wg_ref, wu_ref, wd_ref, gsem, ssem, wsem):
    i = pl.program_id(0)
    n_used = nu_ref[0]
    tm = xbuf.shape[1]

    def weight_copies(e):
        return (pltpu.make_async_copy(wg_hbm.at[e], land_g, wsem.at[0]),
                pltpu.make_async_copy(wu_hbm.at[e], land_u, wsem.at[1]),
                pltpu.make_async_copy(wd_hbm.at[e], land_d, wsem.at[2]))

    @pl.when(i == 0)
    def _():
        for cp in weight_copies(te_ref[0]):
            cp.start()

    @pl.when(rs_ref[i] == 1)
    def _():
        for cp in weight_copies(te_ref[i]):
            cp.wait()
        wg_ref[...] = land_g[...].astype(BF16)
        wu_ref[...] = land_u[...].astype(BF16)
        wd_ref[...] = land_d[...].astype(BF16)

        @pl.when(ne_ref[i] >= 0)
        def _():
            for cp in weight_copies(ne_ref[i]):
                cp.start()

    def gather_copy(tile, slot, r):
        tok = rt_ref[(tile + 1) * tm + r]
        return pltpu.make_async_copy(x_hbm.at[pl.ds(tok, 1)], xbuf.at[slot, pl.ds(r, 1)], gsem.at[slot])

    def scatter_copy(tile, slot, r):
        dst = rd_ref[(tile + 1) * tm + r]
        return pltpu.make_async_copy(ybuf.at[slot, pl.ds(r, 1)], y_hbm.at[pl.ds(dst, 1)], ssem.at[slot])

    def wait_gather(slot):
        pltpu.make_async_copy(x_hbm.at[pl.ds(0, tm)], xbuf.at[slot], gsem.at[slot]).wait()

    def wait_scatter(slot):
        pltpu.make_async_copy(ybuf.at[slot], y_hbm.at[pl.ds(0, tm)], ssem.at[slot]).wait()

    def looped(copy_fn, tile, slot):
        def body(r, carry):
            copy_fn(tile, slot, r).start()
            return carry
        lax.fori_loop(0, tm, body, 0)

    @pl.when(i == 0)
    def _():
        ybuf[...] = jnp.zeros(ybuf.shape, ybuf.dtype)
        looped(gather_copy, 0, 0)

    for cur in range(2):
        nxt = 1 - cur

        @pl.when((i % 2 == cur) & (i <= n_used))
        def _():
            wait_gather(cur)

        @pl.when((i % 2 == cur) & (i >= 1) & (i <= n_used))
        def _():
            wait_scatter(cur)

        @pl.when((i % 2 == cur) & (i < n_used))
        def _():
            x = xbuf[cur].astype(BF16)

            def neighbour_copies(lo, hi):
                for r in range(lo, hi):
                    gather_copy(i + 1, nxt, r).start()
                    scatter_copy(i - 1, nxt, r).start()

            quarter = tm // 4
            neighbour_copies(0, quarter)
            a = _dot(x, wg_ref[...])
            neighbour_copies(quarter, 2 * quarter)
            u = _dot(x, wu_ref[...])
            neighbour_copies(2 * quarter, 3 * quarter)
            hmid = (a * (1.0 / (1.0 + jnp.exp(-a))) * u).astype(BF16)
            neighbour_copies(3 * quarter, tm)
            ybuf[cur] = _dot(hmid, wd_ref[...])

        @pl.when((i % 2 == cur) & (i == n_used))
        def _():
            looped(scatter_copy, i - 1, nxt)
            wait_scatter(nxt)


def expert_mlps(x, w_gate, w_up, w_down, tile_expert, n_used, row_token, row_dest, run_start, next_expert):
    t, d = x.shape
    n_steps = tile_expert.shape[0]
    f = w_gate.shape[2]
    hbm = pl.BlockSpec(memory_space=pl.ANY)
    grid_spec = pltpu.PrefetchScalarGridSpec(
        num_scalar_prefetch=6,
        grid=(n_steps,),
        in_specs=[hbm, hbm, hbm, hbm],
        out_specs=hbm,
        scratch_shapes=[pltpu.VMEM((2, MOE_TILE, d), F32), pltpu.VMEM((2, MOE_TILE, d), F32),
                        pltpu.VMEM((d, f), F32), pltpu.VMEM((d, f), F32), pltpu.VMEM((f, d), F32),
                        pltpu.VMEM((d, f), BF16), pltpu.VMEM((d, f), BF16), pltpu.VMEM((f, d), BF16),
                        pltpu.SemaphoreType.DMA((2,)), pltpu.SemaphoreType.DMA((2,)), pltpu.SemaphoreType.DMA((3,))],
    )
    return pl.pallas_call(
        _moe_kernel,
        grid_spec=grid_spec,
        out_shape=jax.ShapeDtypeStruct((2 * (t + MOE_TILE // 2), d), F32),
        compiler_params=_cparams(("arbitrary",)),
        name="moe_experts",
    )(tile_expert, n_used, row_token, row_dest, run_start, next_expert, x, w_gate, w_up, w_down
      ).reshape(2, t + MOE_TILE // 2, d)


def _combine_kernel(alpha, y1_ref, y2_ref, h_ref, rw_ref, g_ref, b_ref, o_ref, ob_ref):
    rw = rw_ref[...]
    moe = y1_ref[0] * rw[:, 2:3] + y2_ref[0] * rw[:, 3:4]
    y = _ln(alpha * h_ref[...] + moe, g_ref[...], b_ref[...])
    o_ref[...] = y
    ob_ref[...] = y.astype(BF16)


def combine_ln(alpha, y_assign, h, route_out, g, b):
    t, d = h.shape
    tm = ROW_TILE
    row = pl.BlockSpec((tm, d), lambda i: (i, 0))
    vec = pl.BlockSpec((1, d), lambda i: (0, 0))
    return pl.pallas_call(
        functools.partial(_combine_kernel, alpha),
        grid=(t // tm,),
        in_specs=[pl.BlockSpec((1, tm, d), lambda i: (0, i, 0)), pl.BlockSpec((1, tm, d), lambda i: (1, i, 0)), row,
                  pl.BlockSpec((tm, LANES), lambda i: (i, 0)), vec, vec],
        out_specs=[row, row],
        out_shape=[jax.ShapeDtypeStruct((t, d), F32), jax.ShapeDtypeStruct((t, d), BF16)],
        compiler_params=_cparams(("parallel",)),
        name="moe_combine_ln",
    )(y_assign, y_assign, h, route_out, g.reshape(1, d), b.reshape(1, d))


def _combine_final_kernel(alpha, y1_ref, y2_ref, h_ref, rw_ref, g_ref, b_ref, o_ref):
    rw = rw_ref[...]
    moe = y1_ref[...] * rw[:, 2:3] + y2_ref[...] * rw[:, 3:4]
    o_ref[0] = _ln(alpha * h_ref[...] + moe, g_ref[...], b_ref[...])


def combine_ln_final(alpha, y_assign, h, route_out, g, b, bsz, seq):
    t, d = h.shape
    tm = FINAL_TILE
    n_j = (seq - N_META) // tm

    plane = y_assign.shape[1]

    def first_row(bi, j, k=0):
        return pl.multiple_of(k * plane + bi * seq + N_META + j * tm, 8)

    rows, cols, lanes = pl.Element(tm), pl.Element(d), pl.Element(LANES)
    vec = pl.BlockSpec((1, d), lambda bi, j: (0, 0))
    y_assign = y_assign.reshape(2 * plane, d)
    return pl.pallas_call(
        functools.partial(_combine_final_kernel, alpha),
        grid=(bsz, n_j),
        in_specs=[pl.BlockSpec((rows, cols), lambda bi, j: (first_row(bi, j, 0), 0)),
                  pl.BlockSpec((rows, cols), lambda bi, j: (first_row(bi, j, 1), 0)),
                  pl.BlockSpec((rows, cols), lambda bi, j: (first_row(bi, j), 0)),
                  pl.BlockSpec((rows, lanes), lambda bi, j: (first_row(bi, j), 0)), vec, vec],
        out_specs=pl.BlockSpec((1, tm, d), lambda bi, j: (bi, j, 0)),
        out_shape=jax.ShapeDtypeStruct((bsz, seq - N_META, d), F32),
        compiler_params=_cparams(("parallel", "parallel")),
        name="moe_combine_ln_final",
    )(y_assign, y_assign, h, route_out, g.reshape(1, d), b.reshape(1, d))


def _combine_project_kernel(alpha, y1_ref, y2_ref, h_ref, rw_ref, g_ref, b_ref, wa_ref, wb_ref, wc_ref,
                            ba_ref, bb_ref, bc_ref, o_ref, ua_ref, ub_ref, uc_ref, hb_scr):
    i = pl.program_id(0)
    last = pl.num_programs(0) - 1

    def combine(slot):
        rw = rw_ref[...]
        moe = y1_ref[0] * rw[:, 2:3] + y2_ref[0] * rw[:, 3:4]
        y = _ln(alpha * h_ref[...] + moe, g_ref[...], b_ref[...])
        o_ref[...] = y
        hb_scr[slot] = y.astype(BF16)

    def project(slot):
        x = hb_scr[slot]
        ua_ref[...] = _dot(x, wa_ref[0]) + ba_ref[...]
        ub_ref[...] = (_dot(x, wb_ref[0]) + bb_ref[...]).astype(ub_ref.dtype)
        uc_ref[...] = _dot(x, wc_ref[0]) + bc_ref[...]

    @pl.when(i == 0)
    def _():
        combine(0)

    for cur in range(2):
        @pl.when((i % 2 == cur) & (i > 0) & (i < last))
        def _():
            combine(cur)
            project(1 - cur)

        @pl.when((i % 2 == cur) & (i == last))
        def _():
            project(1 - cur)


def combine_ln_project(alpha, y_assign, h, route_out, g, b, w_a, w_b, w_c, layer, b_a, b_b, b_c):
    t, d = h.shape
    tm = FUSED_TILE
    n_tiles = t // tm

    def cur(i):
        return jnp.minimum(i, n_tiles - 1)

    def prev(i):
        return jnp.maximum(i - 1, 0)

    def weight(w):
        return _resident((1, d, w.shape[2]), lambda i: (layer, 0, 0))

    def bias(n):
        return pl.BlockSpec((1, n), lambda i: (0, 0))

    na, nb, nc = w_a.shape[2], w_b.shape[2], w_c.shape[2]
    vec = pl.BlockSpec((1, d), lambda i: (0, 0))
    return pl.pallas_call(
        functools.partial(_combine_project_kernel, alpha),
        grid=(n_tiles + 1,),
        in_specs=[pl.BlockSpec((1, tm, d), lambda i: (0, cur(i), 0)),
                  pl.BlockSpec((1, tm, d), lambda i: (1, cur(i), 0)),
                  pl.BlockSpec((tm, d), lambda i: (cur(i), 0)),
                  pl.BlockSpec((tm, LANES), lambda i: (cur(i), 0)), vec, vec,
                  weight(w_a), weight(w_b), weight(w_c), bias(na), bias(nb), bias(nc)],
        out_specs=[pl.BlockSpec((tm, d), lambda i: (cur(i), 0)),
                   pl.BlockSpec((tm, na), lambda i: (prev(i), 0)),
                   pl.BlockSpec((tm, nb), lambda i: (prev(i), 0)),
                   pl.BlockSpec((tm, nc), lambda i: (prev(i), 0))],
        out_shape=[jax.ShapeDtypeStruct((t, d), F32), jax.ShapeDtypeStruct((t, na), F32),
                   jax.ShapeDtypeStruct((t, nb), BF16), jax.ShapeDtypeStruct((t, nc), F32)],
        scratch_shapes=[pltpu.VMEM((2, tm, d), BF16)],
        compiler_params=_cparams(("arbitrary",)),
        name="combine_ln_in_proj",
    )(y_assign, y_assign, h, route_out, g.reshape(1, d), b.reshape(1, d), w_a, w_b, w_c, b_a, b_b, b_c)


def kernel(x, meta, emb_ln_g, emb_ln_b, w_in, b_in, gla_w2_f, gla_b_f, gla_w2_b, gla_b_b, gla_norm_g,
           swa_sink, swa_norm_g, hy_conv_w, hy_conv_b, hy_w1, hy_b1, hy_freq, hy_w2, hy_b2, hy_w3, hy_skip,
           hy_norm_g, w_out, b_out, ln1_g, ln1_b, router_wg, router_bg, router_we, router_be,
           exp_w_gate, exp_w_up, exp_w_down, ln2_g, ln2_b):
    bsz, seq_in, d = x.shape
    depth = w_in.shape[0]
    seq = seq_in + N_META
    t = bsz * seq
    alpha = (2.0 * depth) ** 0.25

    o_gates, o_r, o_swa, o_hy = IN_GATES, IN_R, IN_SWA, IN_HY
    pad_cols = A_COLS - (o_swa)
    w_a, w_b, w_c = regroup_in_weights(w_in)
    b_a = jnp.concatenate([b_in[:, :o_gates], b_in[:, o_r:o_swa], b_in[:, o_gates:o_r],
                           jnp.zeros((depth, pad_cols), b_in.dtype)], axis=-1)
    q_scale = jnp.concatenate([jnp.full((SWA_WIDTH,), SWA_Q_SCALE, F32), jnp.ones((o_hy - o_swa - SWA_WIDTH,), F32)])
    b_b = b_in[:, o_swa:o_hy] * q_scale
    w_out_b = w_out.astype(BF16)
    w2f = jnp.zeros((depth, LANES, gla_w2_f.shape[2]), F32).at[:, :GLA_RANK].set(gla_w2_f)
    w2b = jnp.zeros((depth, LANES, gla_w2_b.shape[2]), F32).at[:, GLA_RANK:2 * GLA_RANK].set(gla_w2_b)
    n_r = N_GROUPS + N_EXPERTS
    w_router = jnp.zeros((depth, d, LANES), F32).at[:, :, :N_GROUPS].set(router_wg)
    w_router = w_router.at[:, :, N_GROUPS:n_r].set(router_we).astype(BF16)
    b_router = jnp.zeros((depth, 1, LANES), F32).at[:, 0, :N_GROUPS].set(router_bg)
    b_router = b_router.at[:, 0, N_GROUPS:n_r].set(router_be)
    d_exp = exp_w_gate.shape[-1]
    wg_all = exp_w_gate.reshape(depth * N_EXPERTS, d, d_exp)
    wu_all = exp_w_up.reshape(depth * N_EXPERTS, d, d_exp)
    wd_all = exp_w_down.reshape(depth * N_EXPERTS, d_exp, d)

    z_np, rates_np, cos_np, sin_np = _hyena_tables(seq)
    cos_t = jnp.asarray(cos_np).astype(BF16)
    sin_t = jnp.asarray(sin_np).astype(BF16)
    w1p = jnp.zeros((depth, LANES, HY_FFN), F32).at[:, :HY_EMB].set(hy_w1)
    h_sum, h_diff = hyena_filters(jnp.asarray(z_np), jnp.asarray(rates_np), w1p, hy_b1.reshape(depth, 1, HY_FFN),
                                  hy_freq, hy_w2, hy_b2.reshape(depth, 1, HY_FFN), hy_w3)
    k_cos, k_sin = hyena_spectrum(cos_t, sin_t, h_sum, h_diff)

    n_steps = (2 * t + N_EXPERTS * (MOE_TILE - 1)) // MOE_TILE + 2

    tokens = jnp.concatenate([jnp.broadcast_to(meta.astype(x.dtype)[None], (bsz, N_META, d)), x], axis=1)
    h, hb = ln_rows(tokens.reshape(t, d), emb_ln_g, emb_ln_b)
    b_c = b_in[:, o_hy:]
    u_a = project(hb, w_a, 0, b_a[0].reshape(1, -1), F32)
    u_b = project(hb, w_b, 0, b_b[0].reshape(1, -1), BF16)
    u_c = project(hb, w_c, 0, b_c[0].reshape(1, -1), F32)
    for l in range(depth):
        u_a, u_b, u_c = (u.reshape(bsz, seq, -1) for u in (u_a, u_b, u_c))
        y_a = gla_branch(u_a, w2f[l], gla_b_f[l], w2b[l], gla_b_b[l], gla_norm_g[l])
        y_b = swa_branch(u_b, swa_sink[l], swa_norm_g[l])
        y_c = hyena_branch(u_c, hy_conv_w[l], hy_conv_b[l].reshape(1, -1), cos_t, sin_t, k_cos, k_sin, l,
                           hy_skip[l], hy_norm_g[l])
        h1, routed = out_projection(alpha, y_a.reshape(t, -1), y_b.reshape(t, -1), y_c.reshape(t, -1), h,
                                    w_out_b, l, b_out[l], ln1_g[l], ln1_b[l], w_router, b_router)
        expert_ids = routed[:, 0:2].astype(jnp.int32)
        row_token, row_dest, tile_expert, n_used, run_start, next_expert = dispatch_plan(expert_ids, n_steps)
        base = l * N_EXPERTS
        y_assign = expert_mlps(h1, wg_all, wu_all, wd_all, tile_expert + base, n_used, row_token, row_dest,
                               run_start, jnp.where(next_expert >= 0, next_expert + base, -1))
        if l + 1 < depth:
            h, u_a, u_b, u_c = combine_ln_project(
                alpha, y_assign, h1, routed, ln2_g[l], ln2_b[l], w_a, w_b, w_c, l + 1,
                b_a[l + 1].reshape(1, -1), b_b[l + 1].reshape(1, -1), b_c[l + 1].reshape(1, -1))
    return combine_ln_final(alpha, y_assign, h1, routed, ln2_g[depth - 1], ln2_b[depth - 1], bsz, seq)
```
